```python
import math
import jax, jax.numpy as jnp
from jax import lax
import numpy as np

D_MODEL = 1024
BATCH = 4
SEQ = 8192
DEPTH = 1
DEC_BATCH = 32
DEC_SEQ = 4
PAST_LEN = 16384
PAGE_SIZE = 128

D_CONV = 512
CONV_WIDTH = 31
HEAD_DIM = 64
HEADS_PER_GROUP = 4
GROUPS = ((128, 1), (512, 4), (2048, 16))
N_ATT_HEADS = HEADS_PER_GROUP * len(GROUPS)
D_ATT = N_ATT_HEADS * HEAD_DIM
D_ATT_OUT = HEADS_PER_GROUP * HEAD_DIM
D_FF = ((8 * D_MODEL + 3 * 256 - 1) // (3 * 256)) * 256
D_PLE = 256
BLK = 128
N_IN = 2 * D_CONV + 3 * D_ATT + 2 * D_MODEL
EPS = 1e-6
NEG_INF = -1e30

kernel_name = "hybrid_conformer_dilated_attn_decode_step"


def _rms_norm(x, g):
    xf = x.astype(jnp.float32)
    y = xf * lax.rsqrt(jnp.mean(xf * xf, axis=-1, keepdims=True) + EPS)
    return (y * g.astype(jnp.float32)).astype(x.dtype)


def _layer_norm(x, g, b):
    xf = x.astype(jnp.float32)
    mu = jnp.mean(xf, axis=-1, keepdims=True)
    xc = xf - mu
    y = xc * lax.rsqrt(jnp.mean(xc * xc, axis=-1, keepdims=True) + EPS)
    return (y * g.astype(jnp.float32) + b.astype(jnp.float32)).astype(x.dtype)


def _alibi_slopes():
    return 2.0 ** (-8.0 * (jnp.arange(N_ATT_HEADS, dtype=jnp.float32) + 1.0) / N_ATT_HEADS)


def _dilated_prompt(q, k, v, slopes, window, dil):
    B, S, H, Dh = q.shape
    span = dil * BLK
    S_pad = -(-S // span) * span
    nb = S_pad // span
    pad = ((0, 0), (0, S_pad - S), (0, 0), (0, 0))

    def split(a):
        return jnp.pad(a, pad).reshape(B, nb, BLK, dil, H, Dh)

    def with_prev(a):
        prev = jnp.pad(a, ((0, 0), (1, 0), (0, 0), (0, 0), (0, 0), (0, 0)))[:, :-1]
        return jnp.concatenate([prev, a], axis=2)

    qb = split(q)
    kk = with_prev(split(k))
    vv = with_prev(split(v))
    s = jnp.einsum('bnqrhd,bnkrhd->bnrhqk', qb, kk,
                   preferred_element_type=jnp.float32) * (1.0 / math.sqrt(Dh))
    qi = jnp.arange(BLK)[:, None]
    ki = jnp.arange(2 * BLK)[None, :]
    dist = qi + BLK - ki
    band = (dist >= 0) & (dist <= window // dil)
    first = (jnp.arange(nb)[:, None, None] > 0) | (ki >= BLK)
    mask = band[None] & first
    bias = -slopes[:, None, None] * (dist * dil).astype(jnp.float32)[None]
    s = jnp.where(mask[None, :, None, None], s + bias[None, None, None], NEG_INF)
    m = jnp.max(s, axis=-1, keepdims=True)
    p = jnp.exp(s - m)
    l = jnp.sum(p, axis=-1, keepdims=True)
    p = p / l
    o = jnp.einsum('bnrhqk,bnkrhd->bnqrhd', p, vv.astype(jnp.float32))
    o = o.reshape(B, S_pad, H, Dh)[:, :S]
    lse = (m + jnp.log(l))[..., 0]
    lse = jnp.transpose(lse, (0, 1, 4, 2, 3)).reshape(B, S_pad, H)[:, :S]
    return o, lse


def _dilated_sample(q, k_all, v_all, slopes, window, dil, L):
    T = q.shape[1]
    Dh = q.shape[-1]
    j = jnp.arange(window // dil + 1)
    idx = L + jnp.arange(T)[:, None] - j[None, :] * dil
    valid = idx >= 0
    idx_c = jnp.maximum(idx, 0)
    kg = k_all[:, idx_c]
    vg = v_all[:, idx_c]
    s = jnp.einsum('bthd,btjhd->bhtj', q, kg,
                   preferred_element_type=jnp.float32) * (1.0 / math.sqrt(Dh))
    bias = -slopes[:, None, None] * (j * dil).astype(jnp.float32)[None, None, :]
    s = jnp.where(valid[None, None], s + bias[None], NEG_INF)
    m = jnp.max(s, axis=-1, keepdims=True)
    p = jnp.exp(s - m)
    l = jnp.sum(p, axis=-1, keepdims=True)
    p = p / l
    o = jnp.einsum('bhtj,btjhd->bthd', p, vg.astype(jnp.float32))
    lse = jnp.transpose((m + jnp.log(l))[..., 0], (0, 2, 1))
    return o, lse


def _mixer(h, conv_buf, kv_bufs, w_in, w_dw, b_dw, ln_g, ln_b, w_conv_out, w_att_out, w_out):
    B, S, _ = h.shape
    z = h @ w_in
    o_q = 2 * D_CONV
    o_k = o_q + D_ATT
    o_v = o_k + D_ATT
    o_g = o_v + D_ATT
    u = z[..., :D_CONV] * jax.nn.sigmoid(z[..., D_CONV:o_q])
    q = z[..., o_q:o_k].reshape(B, S, N_ATT_HEADS, HEAD_DIM)
    k = z[..., o_k:o_v].reshape(B, S, N_ATT_HEADS, HEAD_DIM)
    v = z[..., o_v:o_g].reshape(B, S, N_ATT_HEADS, HEAD_DIM)
    gate_c = jax.nn.sigmoid(z[..., o_g:o_g + D_MODEL])
    gate_a = jax.nn.sigmoid(z[..., o_g + D_MODEL:])

    if conv_buf is None:
        conv_buf = jnp.zeros((B, CONV_WIDTH - 1, D_CONV), u.dtype)
    xin = jnp.concatenate([conv_buf.astype(u.dtype), u], axis=1)
    new_conv = xin[:, -(CONV_WIDTH - 1):]
    c = lax.conv_general_dilated(xin, w_dw[:, None, :], (1,), 'VALID',
                                 dimension_numbers=('NWC', 'WIO', 'NWC'),
                                 feature_group_count=D_CONV) + b_dw
    c = jax.nn.silu(_layer_norm(c, ln_g, ln_b))
    y_conv = c @ w_conv_out

    slopes = _alibi_slopes()
    outs, lses, new_kv = [], [], []
    for g, (window, dil) in enumerate(GROUPS):
        hs = slice(g * HEADS_PER_GROUP, (g + 1) * HEADS_PER_GROUP)
        qg, kg, vg = q[:, :, hs], k[:, :, hs], v[:, :, hs]
        kv_new = jnp.stack([kg, vg], axis=2)
        if kv_bufs is None:
            o, lse = _dilated_prompt(qg, kg, vg, slopes[hs], window, dil)
            new_kv.append(kv_new[:, -min(window, S):])
        else:
            buf = kv_bufs[g].astype(kv_new.dtype)
            L = buf.shape[1]
            kv_all = jnp.concatenate([buf, kv_new], axis=1)
            o, lse = _dilated_sample(qg, kv_all[:, :, 0], kv_all[:, :, 1], slopes[hs], window, dil, L)
            new_kv.append(kv_all[:, -L:])
        outs.append(o)
        lses.append(lse)
    wts = jax.nn.softmax(jnp.stack(lses, axis=0), axis=0)
    o = jnp.sum(wts[..., None] * jnp.stack(outs, axis=0), axis=0)
    y_att = o.reshape(B, S, D_ATT_OUT).astype(h.dtype) @ w_att_out

    mix = (gate_c * y_conv + gate_a * y_att) @ w_out
    return mix, new_conv, new_kv


def _layer(x, p, conv_buf, kv_bufs, g_mix, w_in, w_dw, b_dw, ln_g, ln_b, w_conv_out, w_att_out,
           w_out, g_ffn, w_ffn_gate, w_ffn_up, w_ffn_down, g_ple, w_ple_gate, w_ple):
    mix, new_conv, new_kv = _mixer(_rms_norm(x, g_mix), conv_buf, kv_bufs, w_in, w_dw, b_dw,
                                   ln_g, ln_b, w_conv_out, w_att_out, w_out)
    x = x + mix
    hf = _rms_norm(x, g_ffn)
    x = x + (jax.nn.silu(hf @ w_ffn_gate) * (hf @ w_ffn_up)) @ w_ffn_down
    x = x + jax.nn.sigmoid(_rms_norm(x, g_ple) @ w_ple_gate) * (p.astype(x.dtype) @ w_ple)
    return x, new_conv, new_kv


def setup_inputs(seed: int = 0) -> dict:
    key = jax.random.key(seed)
    ks = iter(jax.random.split(key, 40))
    f32 = jnp.float32

    def nrm(shape, scale):
        return jax.random.normal(next(ks), shape, f32) * scale

    def gain(shape):
        return 1.0 + 0.01 * jax.random.normal(next(ks), shape, f32)

    d = {}
    d["x_prompt"] = nrm((BATCH, SEQ, D_MODEL), 1.0)
    d["x_sample"] = nrm((DEC_BATCH, DEC_SEQ, D_MODEL), 1.0)
    d["state_conv"] = nrm((DEPTH, DEC_BATCH, CONV_WIDTH - 1, D_CONV), 0.5)
    for g, (window, _) in enumerate(GROUPS):
        d["cache_kv_g%d" % g] = nrm((DEPTH, DEC_BATCH, min(window, PAST_LEN), 2, HEADS_PER_GROUP, HEAD_DIM), 1.0)
    d["p_prompt"] = nrm((DEPTH, BATCH, SEQ, D_PLE), 1.0)
    d["p_sample"] = nrm((DEPTH, DEC_BATCH, DEC_SEQ, D_PLE), 1.0)
    d["g_mix"] = gain((DEPTH, D_MODEL))
    d["w_in"] = nrm((DEPTH, D_MODEL, N_IN), D_MODEL ** -0.5)
    d["w_dw"] = nrm((DEPTH, CONV_WIDTH, D_CONV), CONV_WIDTH ** -0.5)
    d["b_dw"] = nrm((DEPTH, D_CONV), 0.01)
    d["ln_g"] = gain((DEPTH, D_CONV))
    d["ln_b"] = nrm((DEPTH, D_CONV), 0.01)
    d["w_conv_out"] = nrm((DEPTH, D_CONV, D_MODEL), D_CONV ** -0.5)
    d["w_att_out"] = nrm((DEPTH, D_ATT_OUT, D_MODEL), D_ATT_OUT ** -0.5)
    d["w_out"] = nrm((DEPTH, D_MODEL, D_MODEL), D_MODEL ** -0.5)
    d["g_ffn"] = gain((DEPTH, D_MODEL))
    d["w_ffn_gate"] = nrm((DEPTH, D_MODEL, D_FF), D_MODEL ** -0.5)
    d["w_ffn_up"] = nrm((DEPTH, D_MODEL, D_FF), D_MODEL ** -0.5)
    d["w_ffn_down"] = nrm((DEPTH, D_FF, D_MODEL), D_FF ** -0.5)
    d["g_ple"] = gain((DEPTH, D_MODEL))
    d["w_ple_gate"] = nrm((DEPTH, D_MODEL, D_MODEL), D_MODEL ** -0.5)
    d["w_ple"] = nrm((DEPTH, D_PLE, D_MODEL), D_PLE ** -0.5)
    d["g_final"] = gain((D_MODEL,))
    return d


def reference(x_prompt, x_sample, state_conv, cache_kv_g0, cache_kv_g1, cache_kv_g2, p_prompt, p_sample,
              g_mix, w_in, w_dw, b_dw, ln_g, ln_b, w_conv_out, w_att_out, w_out, g_ffn, w_ffn_gate,
              w_ffn_up, w_ffn_down, g_ple, w_ple_gate, w_ple, g_final):
    xp, xs = x_prompt, x_sample
    conv_p, conv_s = [], []
    kv_p = [[], [], []]
    kv_s = [[], [], []]
    for i in range(DEPTH):
        lw = (g_mix[i], w_in[i], w_dw[i], b_dw[i], ln_g[i], ln_b[i], w_conv_out[i], w_att_out[i],
              w_out[i], g_ffn[i], w_ffn_gate[i], w_ffn_up[i], w_ffn_down[i], g_ple[i], w_ple_gate[i], w_ple[i])
        xp, ncp, nkvp = _layer(xp, p_prompt[i], None, None, *lw)
        xs, ncs, nkvs = _layer(xs, p_sample[i], state_conv[i],
                               (cache_kv_g0[i], cache_kv_g1[i], cache_kv_g2[i]), *lw)
        conv_p.append(ncp)
        conv_s.append(ncs)
        for g in range(len(GROUPS)):
            kv_p[g].append(nkvp[g])
            kv_s[g].append(nkvs[g])
    y_prompt = _rms_norm(xp, g_final)
    y_sample = _rms_norm(xs, g_final)
    return (y_prompt, y_sample, jnp.stack(conv_p), jnp.stack(conv_s),
            jnp.stack(kv_p[0]), jnp.stack(kv_s[0]), jnp.stack(kv_p[1]), jnp.stack(kv_s[1]),
            jnp.stack(kv_p[2]), jnp.stack(kv_s[2]))
```

```python
import functools

import jax
import jax.numpy as jnp
from jax import lax
from jax.experimental import pallas as pl
from jax.experimental.pallas import tpu as pltpu

D_MODEL = 1024
D_CONV = 512
CONV_WIDTH = 31
HEAD_DIM = 64
HEADS_PER_GROUP = 4
GROUPS = ((128, 1), (512, 4), (2048, 16))
N_GROUPS = len(GROUPS)
N_ATT_HEADS = HEADS_PER_GROUP * N_GROUPS
D_GROUP = HEADS_PER_GROUP * HEAD_DIM
D_ATT = N_ATT_HEADS * HEAD_DIM
D_FF = 2816
D_PLE = 256
BLK = 128
EPS = 1e-6
NEG_INF = -1e30

LANES = 128
SPAN = 2048
FF_CHUNK = 256
HALO_ROWS = 32
VMEM_LIMIT_BYTES = 56 * 1024 * 1024

BF16 = jnp.bfloat16
F32 = jnp.float32


def _rms(x, g):
    return x * lax.rsqrt(jnp.mean(x * x, axis=-1, keepdims=True) + EPS) * g


def _dot(a, b):
    return jnp.dot(a, b, preferred_element_type=F32)


def _dot_nt(a, b):
    return lax.dot_general(a, b, (((1,), (1,)), ((), ())), preferred_element_type=F32)


def _vmem_spec():
    return pl.BlockSpec(memory_space=pltpu.VMEM)


def _inproj_kernel(x_ref, g_ref, wglu_ref, wq_ref, wk_ref, wv_ref, u_ref, *out_refs, tm, n_tiles, tails):
    q_refs = out_refs[0:N_GROUPS]
    kv_refs = out_refs[N_GROUPS : 2 * N_GROUPS]
    tail_refs = out_refs[2 * N_GROUPS :]
    i = pl.program_id(1)
    h = _rms(x_ref[...], g_ref[...]).astype(BF16)
    a = _dot(h, wglu_ref[:, :D_CONV])
    b = _dot(h, wglu_ref[:, D_CONV:])
    u_ref[...] = a * jax.nn.sigmoid(b)
    for g, (window, _) in enumerate(GROUPS):
        cols = slice(g * D_GROUP, (g + 1) * D_GROUP)
        q = _dot(h, wq_ref[:, cols]) * (HEAD_DIM**-0.5)
        k = _dot(h, wk_ref[:, cols])
        v = _dot(h, wv_ref[:, cols])
        q_refs[g][...] = q.astype(q_refs[g].dtype)
        kv_refs[g][:, :D_GROUP] = k.astype(kv_refs[g].dtype)
        kv_refs[g][:, D_GROUP:] = v.astype(kv_refs[g].dtype)
        if tails:
            t_ref = tail_refs[g]
            if window <= tm:

                @pl.when(i == n_tiles - 1)
                def _():
                    t_ref[:, :D_GROUP] = k[tm - window :]
                    t_ref[:, D_GROUP:] = v[tm - window :]

            else:

                @pl.when(i >= n_tiles - window // tm)
                def _():
                    t_ref[:, :D_GROUP] = k
                    t_ref[:, D_GROUP:] = v


def _inproj(x, g_mix, wglu, wq, wk, wv, *, tm, qkv_dtype, tails):
    bsz, seq, _ = x.shape
    n_tiles = seq // tm
    assert seq % tm == 0

    def row_spec(cols):
        return pl.BlockSpec((None, tm, cols), lambda b, i: (b, i, 0))

    out_shape = [jax.ShapeDtypeStruct((bsz, seq, D_CONV), F32)]
    out_specs = [row_spec(D_CONV)]
    out_shape += [jax.ShapeDtypeStruct((bsz, seq, D_GROUP), qkv_dtype)] * N_GROUPS
    out_specs += [row_spec(D_GROUP)] * N_GROUPS
    out_shape += [jax.ShapeDtypeStruct((bsz, seq, 2 * D_GROUP), qkv_dtype)] * N_GROUPS
    out_specs += [row_spec(2 * D_GROUP)] * N_GROUPS
    if tails:
        for window, _ in GROUPS:
            assert window <= seq
            out_shape.append(jax.ShapeDtypeStruct((bsz, window, 2 * D_GROUP), F32))
            if window <= tm:
                out_specs.append(pl.BlockSpec((None, window, 2 * D_GROUP), lambda b, i: (b, 0, 0)))
            else:
                assert window % tm == 0
                first = n_tiles - window // tm
                out_specs.append(
                    pl.BlockSpec(
                        (None, tm, 2 * D_GROUP),
                        lambda b, i, first=first: (b, jnp.maximum(i - first, 0), 0),
                    )
                )
    return pl.pallas_call(
        functools.partial(_inproj_kernel, tm=tm, n_tiles=n_tiles, tails=tails),
        grid=(bsz, n_tiles),
        in_specs=[row_spec(D_MODEL), _vmem_spec(), _vmem_spec(), _vmem_spec(), _vmem_spec(), _vmem_spec()],
        out_specs=out_specs,
        out_shape=out_shape,
        compiler_params=pltpu.CompilerParams(
            dimension_semantics=("arbitrary", "arbitrary"), vmem_limit_bytes=VMEM_LIMIT_BYTES
        ),
        name="inproj_tails" if tails else "inproj",
    )(x, g_mix, wglu, wq, wk, wv)


def _prompt_attn_kernel(slopes_ref, *refs):
    q_refs = refs[0:3]
    kv_refs = refs[3:6]
    kvp_refs = refs[6:9]
    o_ref = refs[9]
    qs_ref, kvs_ref, oscr_ref, lscr_ref = refs[10:14]
    n = pl.program_id(1)
    n_blocks = SPAN // BLK

    qi = lax.broadcasted_iota(jnp.int32, (BLK, 2 * BLK), 0)
    ki = lax.broadcasted_iota(jnp.int32, (BLK, 2 * BLK), 1)
    dist = qi + BLK - ki
    band = (dist >= 0) & (dist <= BLK)
    dist_f = dist.astype(F32)
    low_lanes = lax.broadcasted_iota(jnp.int32, (BLK, LANES), 1) < HEAD_DIM

    for g, (_, dil) in enumerate(GROUPS):
        nb = n_blocks // dil
        q_ref, kv_ref, kvp_ref = q_refs[g], kv_refs[g], kvp_refs[g]
        for r in range(dil):
            for jj in range(nb):
                bi = r * nb + jj
                kv_cols = slice(r * 2 * D_GROUP, (r + 1) * 2 * D_GROUP)
                qs_ref[bi] = q_ref[jj * BLK : (jj + 1) * BLK, r * D_GROUP : (r + 1) * D_GROUP]
                if jj == 0:
                    kvs_ref[bi, 0:BLK] = kvp_ref[:, kv_cols]
                else:
                    kvs_ref[bi, 0:BLK] = kv_ref[(jj - 1) * BLK : jj * BLK, kv_cols]
                kvs_ref[bi, BLK : 2 * BLK] = kv_ref[jj * BLK : (jj + 1) * BLK, kv_cols]

        def block_body(bi, carry, g=g, dil=dil, nb=nb):
            jj = bi % nb
            r = bi // nb
            is_first = jnp.logical_and(n == 0, jj == 0).astype(jnp.int32)
            valid = band & (ki >= BLK * is_first)
            row0 = jj * (BLK * dil) + r
            if dil == 1:
                rows = pl.ds(pl.multiple_of(row0, BLK), BLK)
            else:
                rows = pl.ds(row0, BLK, stride=dil)
            for hp in range(2):
                lanes = slice(hp * LANES, (hp + 1) * LANES)
                q2 = qs_ref[bi, :, lanes]
                k2 = kvs_ref[bi, :, lanes]
                v2 = kvs_ref[bi, :, D_GROUP + hp * LANES : D_GROUP + (hp + 1) * LANES]
                outs, lses = [], []
                for hh in range(2):
                    head_lanes = low_lanes if hh == 0 else jnp.logical_not(low_lanes)
                    qm = jnp.where(head_lanes, q2, jnp.zeros_like(q2))
                    s = _dot_nt(qm, k2)
                    slope = slopes_ref[g * HEADS_PER_GROUP + hp * 2 + hh] * float(dil)
                    s = jnp.where(valid, s - dist_f * slope, NEG_INF)
                    m = jnp.max(s, axis=-1, keepdims=True)
                    p = jnp.exp(s - m)
                    l = jnp.sum(p, axis=-1, keepdims=True)
                    pv = _dot(p.astype(BF16), v2)
                    outs.append(pv / l)
                    lses.append(jnp.broadcast_to(m + jnp.log(l), (BLK, LANES)))
                oscr_ref[g, hp, rows, :] = jnp.where(low_lanes, outs[0], outs[1])
                lscr_ref[g, hp, rows, :] = jnp.where(low_lanes, lses[0], lses[1])
            return carry

        lax.fori_loop(0, n_blocks, block_body, 0)

    chunk = 256

    def combine_body(c, carry):
        rows = pl.ds(pl.multiple_of(c * chunk, chunk), chunk)
        for hp in range(2):
            l0 = lscr_ref[0, hp, rows, :]
            l1 = lscr_ref[1, hp, rows, :]
            l2 = lscr_ref[2, hp, rows, :]
            m = jnp.maximum(jnp.maximum(l0, l1), l2)
            w0 = jnp.exp(l0 - m)
            w1 = jnp.exp(l1 - m)
            w2 = jnp.exp(l2 - m)
            num = w0 * oscr_ref[0, hp, rows, :] + w1 * oscr_ref[1, hp, rows, :] + w2 * oscr_ref[2, hp, rows, :]
            o_ref[rows, hp * LANES : (hp + 1) * LANES] = (num / (w0 + w1 + w2)).astype(o_ref.dtype)
        return carry

    lax.fori_loop(0, SPAN // chunk, combine_body, 0)


def _prompt_attention(slopes, qs, kvs):
    bsz, seq, _ = qs[0].shape
    assert seq % SPAN == 0
    n_blocks = SPAN // BLK
    q_views, kv_views, q_specs, kv_specs, kvp_specs = [], [], [], [], []
    for g, (_, dil) in enumerate(GROUPS):
        q_views.append(qs[g].reshape(bsz, seq // dil, dil * D_GROUP))
        kv_views.append(kvs[g].reshape(bsz, seq // dil, dil * 2 * D_GROUP))
        q_specs.append(pl.BlockSpec((None, SPAN // dil, dil * D_GROUP), lambda b, n: (b, n, 0)))
        kv_specs.append(pl.BlockSpec((None, SPAN // dil, dil * 2 * D_GROUP), lambda b, n: (b, n, 0)))
        per_span = n_blocks // dil
        kvp_specs.append(
            pl.BlockSpec(
                (None, BLK, dil * 2 * D_GROUP),
                lambda b, n, per_span=per_span: (b, jnp.maximum(n * per_span - 1, 0), 0),
            )
        )
    return pl.pallas_call(
        _prompt_attn_kernel,
        grid=(bsz, seq // SPAN),
        in_specs=[pl.BlockSpec(memory_space=pltpu.SMEM)] + q_specs + kv_specs + kvp_specs,
        out_specs=pl.BlockSpec((None, SPAN, D_GROUP), lambda b, n: (b, n, 0)),
        out_shape=jax.ShapeDtypeStruct((bsz, seq, D_GROUP), BF16),
        scratch_shapes=[
            pltpu.VMEM((n_blocks, BLK, D_GROUP), BF16),
            pltpu.VMEM((n_blocks, 2 * BLK, 2 * D_GROUP), BF16),
            pltpu.VMEM((N_GROUPS, 2, SPAN, LANES), F32),
            pltpu.VMEM((N_GROUPS, 2, SPAN, LANES), F32),
        ],
        compiler_params=pltpu.CompilerParams(
            dimension_semantics=("arbitrary", "arbitrary"), vmem_limit_bytes=VMEM_LIMIT_BYTES
        ),
        name="prompt_attention",
    )(slopes, *q_views, *kv_views, *kv_views)


Q_ROWS = 8


def _sample_attn_kernel(slopes_ref, q_ref, kn0, kn1, kn2, c0, c1, c2, o_ref, pad_ref, *, t_new):
    kn_refs = (kn0, kn1, kn2)
    cache_refs = (c0, c1, c2)
    n_rows = HEADS_PER_GROUP * Q_ROWS
    row = lax.broadcasted_iota(jnp.int32, (n_rows, D_GROUP), 0)
    lane = lax.broadcasted_iota(jnp.int32, (n_rows, D_GROUP), 1)
    q_shift, head_shift, blk_shift = Q_ROWS.bit_length() - 1, HEAD_DIM.bit_length() - 1, BLK.bit_length() - 1
    head_sel = (row >> q_shift) == (lane >> head_shift)

    def pad_rows(x):
        cols = x.shape[-1]
        pad_ref[:, :cols] = jnp.zeros((Q_ROWS, cols), F32)
        pad_ref[0:t_new, :cols] = x
        return pad_ref[:, :cols]

    q_all = pad_rows(q_ref[...])
    o_groups, lse_groups = [], []
    for g, (window, dil) in enumerate(GROUPS):
        n_res = min(dil, t_new)
        assert dil == 1 or dil >= t_new
        kv_new = pad_rows(kn_refs[g][...])
        cache = cache_refs[g]
        k_all = jnp.concatenate(
            [cache[:, r * 2 * D_GROUP : r * 2 * D_GROUP + D_GROUP] for r in range(n_res)] + [kv_new[:, :D_GROUP]],
            axis=0,
        ).astype(BF16)
        v_all = jnp.concatenate(
            [cache[:, r * 2 * D_GROUP + D_GROUP : (r + 1) * 2 * D_GROUP] for r in range(n_res)]
            + [kv_new[:, D_GROUP:]],
            axis=0,
        ).astype(BF16)
        n_keys = n_res * BLK + Q_ROWS
        q_g = q_all[:, g * D_GROUP : (g + 1) * D_GROUP]
        q_exp = jnp.where(head_sel, jnp.concatenate([q_g] * HEADS_PER_GROUP, axis=0), 0.0).astype(BF16)
        s = _dot_nt(q_exp, k_all)
        srow = lax.broadcasted_iota(jnp.int32, (n_rows, n_keys), 0)
        col = lax.broadcasted_iota(jnp.int32, (n_rows, n_keys), 1)
        i_q = srow & (Q_ROWS - 1)
        in_cache = col < n_res * BLK
        pos = jnp.where(in_cache, (col & (BLK - 1)) * dil + (col >> blk_shift), window + (col - n_res * BLK))
        real = in_cache | (col - n_res * BLK < t_new)
        d_tok = window + i_q - pos
        valid = real & (d_tok >= 0) & ((d_tok & (dil - 1)) == 0) & (d_tok <= window)
        slope = jnp.zeros((n_rows, 1), F32)
        hrow = lax.broadcasted_iota(jnp.int32, (n_rows, 1), 0) >> q_shift
        for hh in range(HEADS_PER_GROUP):
            slope = jnp.where(hrow == hh, slopes_ref[g * HEADS_PER_GROUP + hh], slope)
        s = jnp.where(valid, s - slope * d_tok.astype(F32), NEG_INF)
        m = jnp.max(s, axis=-1, keepdims=True)
        p = jnp.exp(s - m)
        l = jnp.sum(p, axis=-1, keepdims=True)
        o_groups.append(_dot(p.astype(BF16), v_all) / l)
        lse_groups.append(m + jnp.log(l))
    m = jnp.maximum(jnp.maximum(lse_groups[0], lse_groups[1]), lse_groups[2])
    ws = [jnp.exp(x - m) for x in lse_groups]
    o = (ws[0] * o_groups[0] + ws[1] * o_groups[1] + ws[2] * o_groups[2]) / (ws[0] + ws[1] + ws[2])
    o = jnp.where(head_sel, o, 0.0)
    o8 = o[0:Q_ROWS]
    for hh in range(1, HEADS_PER_GROUP):
        o8 = o8 + o[hh * Q_ROWS : (hh + 1) * Q_ROWS]
    o_ref[...] = o8[0:t_new]


def _sample_attention(slopes, q, kv_new, caches):
    bd, t_new, _ = q.shape
    assert t_new <= Q_ROWS
    cache_views, cache_specs = [], []
    for g, (window, dil) in enumerate(GROUPS):
        assert caches[g].shape[1] == window
        n_res = min(dil, t_new)
        cache_views.append(caches[g].reshape(bd, window // dil, dil * 2 * D_GROUP))
        cache_specs.append(pl.BlockSpec((None, window // dil, n_res * 2 * D_GROUP), lambda b: (b, 0, 0)))
    new_spec = pl.BlockSpec((None, t_new, 2 * D_GROUP), lambda b: (b, 0, 0))
    return pl.pallas_call(
        functools.partial(_sample_attn_kernel, t_new=t_new),
        grid=(bd,),
        in_specs=[pl.BlockSpec(memory_space=pltpu.SMEM), pl.BlockSpec((None, t_new, D_ATT), lambda b: (b, 0, 0))]
        + [new_spec] * N_GROUPS
        + cache_specs,
        out_specs=pl.BlockSpec((None, t_new, D_GROUP), lambda b: (b, 0, 0)),
        out_shape=jax.ShapeDtypeStruct((bd, t_new, D_GROUP), F32),
        scratch_shapes=[pltpu.VMEM((Q_ROWS, D_ATT), F32)],
        compiler_params=pltpu.CompilerParams(dimension_semantics=("arbitrary",)),
        name="sample_attention",
    )(slopes, q, *kv_new, *cache_views)


def _sample_conv_kernel(state_ref, u_ref, wdw_ref, bdw_ref, c_ref, xin_ref, *, t_new):
    n_state = CONV_WIDTH - 1
    xin_ref[:, 0:n_state, :] = state_ref[...]
    xin_ref[:, n_state : n_state + t_new, :] = u_ref[...]
    acc = jnp.broadcast_to(bdw_ref[...][None], c_ref.shape)
    for w in range(CONV_WIDTH):
        acc = acc + xin_ref[:, w : w + t_new, :] * wdw_ref[w : w + 1, :][None]
    c_ref[...] = acc


def _sample_conv(state, u, w_dw, b_dw):
    bd, t_new, ch = u.shape
    return pl.pallas_call(
        functools.partial(_sample_conv_kernel, t_new=t_new),
        in_specs=[_vmem_spec()] * 4,
        out_specs=_vmem_spec(),
        out_shape=jax.ShapeDtypeStruct((bd, t_new, ch), F32),
        scratch_shapes=[pltpu.VMEM((bd, CONV_WIDTH - 1 + t_new, ch), F32)],
        name="sample_conv",
    )(state, u, w_dw, b_dw)


CONV_ROWS = 64


def _post_kernel(*refs, tm, conv_in_kernel):
    if conv_in_kernel:
        x_ref, u_ref, uprev_ref, oatt_ref, p_ref = refs[0:5]
        refs = refs[5:]
    else:
        x_ref, c_ref, oatt_ref, p_ref = refs[0:4]
        refs = refs[4:]
    (gmix_ref, wdw_ref, bdw_ref, lng_ref, lnb_ref, wco_ref, wao_ref, wgc_ref, wga_ref, wout_ref, gffn_ref,
     wfg_ref, wfu_ref, wfd_ref, gple_ref, wpg_ref, wple_ref, gfin_ref, y_ref, xin_ref, act_ref, acc_ref) = refs

    x = x_ref[...]
    h = _rms(x, gmix_ref[...]).astype(BF16)

    def ln_swish(c):
        mu = jnp.mean(c, axis=-1, keepdims=True)
        xc = c - mu
        y = xc * lax.rsqrt(jnp.mean(xc * xc, axis=-1, keepdims=True) + EPS)
        y = y * lng_ref[...] + lnb_ref[...]
        return (y * jax.nn.sigmoid(y)).astype(BF16)

    if conv_in_kernel:
        i = pl.program_id(1)
        prev = uprev_ref[...]
        xin_ref[0:HALO_ROWS] = jnp.where(i > 0, prev, jnp.zeros_like(prev))
        xin_ref[HALO_ROWS:] = u_ref[...]
        off = HALO_ROWS - (CONV_WIDTH - 1)
        for rb in range(tm // CONV_ROWS):
            acc = jnp.broadcast_to(bdw_ref[...], (CONV_ROWS, D_CONV))
            for w in range(CONV_WIDTH):
                acc = acc + xin_ref[pl.ds(off + w + rb * CONV_ROWS, CONV_ROWS), :] * wdw_ref[w : w + 1, :]
            act_ref[rb * CONV_ROWS : (rb + 1) * CONV_ROWS] = ln_swish(acc)
    else:
        act_ref[...] = ln_swish(c_ref[...])

    act = act_ref[...]
    oatt = oatt_ref[...].astype(BF16)
    y_conv = _dot(act, wco_ref[...])
    y_att = _dot(oatt, wao_ref[...])
    gate_c = jax.nn.sigmoid(_dot(h, wgc_ref[...]))
    gate_a = jax.nn.sigmoid(_dot(h, wga_ref[...]))
    x1 = x + _dot((gate_c * y_conv + gate_a * y_att).astype(BF16), wout_ref[...])

    hf = _rms(x1, gffn_ref[...]).astype(BF16)
    acc_ref[...] = jnp.zeros_like(acc_ref)

    def ffn_body(c, carry):
        gate = _dot(hf, wfg_ref[c])
        up = _dot(hf, wfu_ref[c])
        hidden = (gate * jax.nn.sigmoid(gate) * up).astype(BF16)
        acc_ref[...] += _dot(hidden, wfd_ref[c])
        return carry

    lax.fori_loop(0, D_FF // FF_CHUNK, ffn_body, 0)
    x2 = x1 + acc_ref[...]

    hp = _rms(x2, gple_ref[...]).astype(BF16)
    x3 = x2 + jax.nn.sigmoid(_dot(hp, wpg_ref[...])) * _dot(p_ref[...].astype(BF16), wple_ref[...])
    y_ref[...] = _rms(x3, gfin_ref[...])


def _post(x, conv_in, oatt, p, weights, *, tm, conv_in_kernel):
    bsz, seq, _ = x.shape
    assert seq % tm == 0 and tm % CONV_ROWS == 0 and tm % HALO_ROWS == 0

    def row_spec(cols):
        return pl.BlockSpec((None, tm, cols), lambda b, i: (b, i, 0))

    data = [x, conv_in]
    data_specs = [row_spec(D_MODEL), row_spec(D_CONV)]
    if conv_in_kernel:
        per_tile = tm // HALO_ROWS
        data.append(conv_in)
        data_specs.append(
            pl.BlockSpec((None, HALO_ROWS, D_CONV), lambda b, i: (b, jnp.maximum(i * per_tile - 1, 0), 0))
        )
    data += [oatt, p]
    data_specs += [row_spec(D_GROUP), row_spec(D_PLE)]
    return pl.pallas_call(
        functools.partial(_post_kernel, tm=tm, conv_in_kernel=conv_in_kernel),
        grid=(bsz, seq // tm),
        in_specs=data_specs + [_vmem_spec()] * len(weights),
        out_specs=row_spec(D_MODEL),
        out_shape=jax.ShapeDtypeStruct((bsz, seq, D_MODEL), F32),
        scratch_shapes=[
            pltpu.VMEM((tm + HALO_ROWS, D_CONV), F32),
            pltpu.VMEM((tm, D_CONV), BF16),
            pltpu.VMEM((tm, D_MODEL), F32),
        ],
        compiler_params=pltpu.CompilerParams(
            dimension_semantics=("arbitrary", "arbitrary"), vmem_limit_bytes=VMEM_LIMIT_BYTES
        ),
        name="post_conv" if conv_in_kernel else "post",
    )(*data, *weights)


def _layer(i, x_prompt, x_sample, state_conv, caches, p_prompt, p_sample, g_mix, w_in, w_dw, b_dw, ln_g, ln_b,
           w_conv_out, w_att_out, w_out, g_ffn, w_ffn_gate, w_ffn_up, w_ffn_down, g_ple, w_ple_gate, w_ple,
           g_final_or_none):
    bsz, seq, _ = x_prompt.shape
    bd, t_new, _ = x_sample.shape

    def row(v):
        return v.reshape(1, -1)

    o_q = 2 * D_CONV
    o_k = o_q + D_ATT
    o_v = o_k + D_ATT
    o_g = o_v + D_ATT
    w_in_b = w_in[i].astype(BF16)
    wglu, wq, wk, wv = w_in_b[:, :o_q], w_in_b[:, o_q:o_k], w_in_b[:, o_k:o_v], w_in_b[:, o_v:o_g]
    wgc, wga = w_in_b[:, o_g : o_g + D_MODEL], w_in_b[:, o_g + D_MODEL :]
    n_ff = D_FF // FF_CHUNK
    wfg = w_ffn_gate[i].astype(BF16).reshape(D_MODEL, n_ff, FF_CHUNK).transpose(1, 0, 2)
    wfu = w_ffn_up[i].astype(BF16).reshape(D_MODEL, n_ff, FF_CHUNK).transpose(1, 0, 2)
    wfd = w_ffn_down[i].astype(BF16).reshape(n_ff, FF_CHUNK, D_MODEL)
    post_weights = [
        row(g_mix[i]), w_dw[i], row(b_dw[i]), row(ln_g[i]), row(ln_b[i]), w_conv_out[i].astype(BF16),
        w_att_out[i].astype(BF16), wgc, wga, w_out[i].astype(BF16), row(g_ffn[i]), wfg, wfu, wfd, row(g_ple[i]),
        w_ple_gate[i].astype(BF16), w_ple[i].astype(BF16), row(g_final_or_none),
    ]
    slopes = 2.0 ** (-8.0 * (jnp.arange(N_ATT_HEADS, dtype=F32) + 1.0) / N_ATT_HEADS)

    outs = _inproj(x_prompt, row(g_mix[i]), wglu, wq, wk, wv, tm=512, qkv_dtype=BF16, tails=True)
    u_p, q_p, kv_p, tails_p = outs[0], outs[1:4], outs[4:7], outs[7:10]
    o_p = _prompt_attention(slopes, q_p, kv_p)
    y_p = _post(x_prompt, u_p, o_p, p_prompt[i], post_weights, tm=256, conv_in_kernel=True)
    conv_p = u_p[:, seq - (CONV_WIDTH - 1) :]
    kv_caches_p = [t.reshape(bsz, t.shape[1], 2, HEADS_PER_GROUP, HEAD_DIM) for t in tails_p]

    n_s = bd * t_new
    outs = _inproj(
        x_sample.reshape(1, n_s, D_MODEL), row(g_mix[i]), wglu, wq, wk, wv, tm=n_s, qkv_dtype=F32, tails=False
    )
    u_s = outs[0].reshape(bd, t_new, D_CONV)
    q_s = jnp.concatenate(outs[1:4], axis=-1).reshape(bd, t_new, D_ATT)
    kv_s = [t.reshape(bd, t_new, 2 * D_GROUP) for t in outs[4:7]]
    caches_flat = [c[i].reshape(bd, c.shape[2], 2 * D_GROUP) for c in caches]
    o_s = _sample_attention(slopes, q_s, kv_s, caches_flat)
    c_s = _sample_conv(state_conv[i], u_s, w_dw[i], row(b_dw[i]))
    y_s = _post(
        x_sample.reshape(1, n_s, D_MODEL), c_s.reshape(1, n_s, D_CONV), o_s.reshape(1, n_s, D_GROUP),
        p_sample[i].reshape(1, n_s, D_PLE), post_weights, tm=n_s, conv_in_kernel=False,
    ).reshape(bd, t_new, D_MODEL)
    conv_s = jnp.concatenate([state_conv[i][:, t_new:], u_s], axis=1)
    kv_caches_s = [
        jnp.concatenate([cf[:, t_new:], kn], axis=1).reshape(bd, cf.shape[1], 2, HEADS_PER_GROUP, HEAD_DIM)
        for cf, kn in zip(caches_flat, kv_s)
    ]
    return y_p, y_s, conv_p, conv_s, kv_caches_p, kv_caches_s


def kernel(x_prompt, x_sample, state_conv, cache_kv_g0, cache_kv_g1, cache_kv_g2, p_prompt, p_sample, g_mix, w_in,
           w_dw, b_dw, ln_g, ln_b, w_conv_out, w_att_out, w_out, g_ffn, w_ffn_gate, w_ffn_up, w_ffn_down, g_ple,
           w_ple_gate, w_ple, g_final):
    depth = w_in.shape[0]
    assert depth == 1
    y_p, y_s, conv_p, conv_s, kv_p, kv_s = _layer(
        0, x_prompt, x_sample, state_conv, (cache_kv_g0, cache_kv_g1, cache_kv_g2), p_prompt, p_sample, g_mix, w_in,
        w_dw, b_dw, ln_g, ln_b, w_conv_out, w_att_out, w_out, g_ffn, w_ffn_gate, w_ffn_up, w_ffn_down, g_ple,
        w_ple_gate, w_ple, g_final,
    )
    return (y_p, y_s, conv_p[None], conv_s[None], kv_p[0][None], kv_s[0][None], kv_p[1][None], kv_s[1][None],
            kv_p[2][None], kv_s[2][None])
```

```python
import functools

import jax
import jax.numpy as jnp
from jax import lax
from jax.experimental import pallas as pl
from jax.experimental.pallas import tpu as pltpu

D_MODEL = 1024
D_CONV = 512
CONV_WIDTH = 31
HEAD_DIM = 64
HEADS_PER_GROUP = 4
GROUPS = ((128, 1), (512, 4), (2048, 16))
N_GROUPS = len(GROUPS)
N_ATT_HEADS = HEADS_PER_GROUP * N_GROUPS
D_GROUP = HEADS_PER_GROUP * HEAD_DIM
D_ATT = N_ATT_HEADS * HEAD_DIM
D_FF = 2816
D_PLE = 256
BLK = 128
EPS = 1e-6
NEG_INF = -1e30

LANES = 128
SPAN = 2048
FF_CHUNK = 256
HALO_ROWS = 32
VMEM_LIMIT_BYTES = 56 * 1024 * 1024

BF16 = jnp.bfloat16
F32 = jnp.float32


def _rms(x, g):
    return x * lax.rsqrt(jnp.mean(x * x, axis=-1, keepdims=True) + EPS) * g


def _dot(a, b):
    return jnp.dot(a, b, preferred_element_type=F32)


def _dot_nt(a, b):
    return lax.dot_general(a, b, (((1,), (1,)), ((), ())), preferred_element_type=F32)


def _vmem_spec():
    return pl.BlockSpec(memory_space=pltpu.VMEM)


def _inproj_kernel(x_ref, g_ref, wglu_ref, wq_ref, wk_ref, wv_ref, u_ref, *refs, tm, n_tiles, prompt):
    q_refs = refs[0:N_GROUPS]
    kv_refs = refs[N_GROUPS : 2 * N_GROUPS]
    tail_refs = refs[2 * N_GROUPS : 3 * N_GROUPS] if prompt else ()
    slab_ref = refs[-1] if prompt else None
    i = pl.program_id(1)
    h = _rms(x_ref[...], g_ref[...]).astype(BF16)
    a = _dot(h, wglu_ref[:, :D_CONV])
    b = _dot(h, wglu_ref[:, D_CONV:])
    u_ref[...] = a * jax.nn.sigmoid(b)
    for g, (window, dil) in enumerate(GROUPS):
        cols = slice(g * D_GROUP, (g + 1) * D_GROUP)
        q = _dot(h, wq_ref[:, cols]) * (HEAD_DIM**-0.5)
        k = _dot(h, wk_ref[:, cols])
        v = _dot(h, wv_ref[:, cols])
        q_ref, kv_ref = q_refs[g], kv_refs[g]
        if not prompt or dil == 1:
            q_ref[...] = q.astype(q_ref.dtype)
            kv_ref[:, :D_GROUP] = k.astype(kv_ref.dtype)
            kv_ref[:, D_GROUP:] = v.astype(kv_ref.dtype)
        else:
            n_slab = D_GROUP // LANES
            for j, val in enumerate((q, k, v)):
                for s in range(n_slab):
                    slab_ref[j * n_slab + s] = val[:, s * LANES : (s + 1) * LANES]
            for r in range(dil):
                rows = pl.ds(r, tm // dil, stride=dil)
                for s in range(n_slab):
                    lane0 = s * LANES
                    q_ref[:, r * D_GROUP + lane0 : r * D_GROUP + lane0 + LANES] = slab_ref[s, rows, :].astype(
                        q_ref.dtype
                    )
                    for j in (1, 2):
                        c0 = r * 2 * D_GROUP + (j - 1) * D_GROUP + lane0
                        kv_ref[:, c0 : c0 + LANES] = slab_ref[j * n_slab + s, rows, :].astype(kv_ref.dtype)
        if prompt:
            t_ref = tail_refs[g]
            if window <= tm:

                @pl.when(i == n_tiles - 1)
                def _():
                    t_ref[0] = k[tm - window :].T
                    t_ref[1] = v[tm - window :].T

            else:

                @pl.when(i >= n_tiles - window // tm)
                def _():
                    t_ref[0] = k.T
                    t_ref[1] = v.T


def _inproj(x, g_mix, wglu, wq, wk, wv, *, tm, prompt):
    bsz, seq, _ = x.shape
    n_tiles = seq // tm
    assert seq % tm == 0
    qkv_dtype = BF16 if prompt else F32

    def row_spec(rows, cols):
        return pl.BlockSpec((None, rows, cols), lambda b, i: (b, i, 0))

    out_shape = [jax.ShapeDtypeStruct((bsz, seq, D_CONV), F32)]
    out_specs = [row_spec(tm, D_CONV)]
    dils = [dil if prompt else 1 for _, dil in GROUPS]
    for width in (D_GROUP, 2 * D_GROUP):
        for dil in dils:
            assert tm % (16 * dil) == 0
            out_shape.append(jax.ShapeDtypeStruct((bsz, seq // dil, dil * width), qkv_dtype))
            out_specs.append(row_spec(tm // dil, dil * width))
    scratch = []
    if prompt:
        scratch.append(pltpu.VMEM((3 * D_GROUP // LANES, tm, LANES), F32))
        for window, _ in GROUPS:
            assert window <= seq and (window <= tm or window % tm == 0)
            out_shape.append(jax.ShapeDtypeStruct((bsz, 2, D_GROUP, window), F32))
            first = n_tiles - max(window // tm, 1)
            out_specs.append(
                pl.BlockSpec(
                    (None, 2, D_GROUP, min(window, tm)),
                    lambda b, i, first=first: (b, 0, 0, jnp.maximum(i - first, 0)),
                )
            )
    return pl.pallas_call(
        functools.partial(_inproj_kernel, tm=tm, n_tiles=n_tiles, prompt=prompt),
        grid=(bsz, n_tiles),
        in_specs=[row_spec(tm, D_MODEL)] + [_vmem_spec()] * 5,
        out_specs=out_specs,
        out_shape=out_shape,
        scratch_shapes=scratch,
        compiler_params=pltpu.CompilerParams(
            dimension_semantics=("arbitrary", "arbitrary"), vmem_limit_bytes=VMEM_LIMIT_BYTES
        ),
        name="inproj_prompt" if prompt else "inproj_decode",
    )(x, g_mix, wglu, wq, wk, wv)


ATTN_UNROLL = 4


def _prompt_attn_kernel(slopes_ref, *refs):
    q_refs = refs[0:3]
    kv_refs = refs[3:6]
    kvp_refs = refs[6:9]
    o_ref = refs[9]
    qs_ref, kvs_ref, oscr_ref, lscr_ref = refs[10:14]
    n = pl.program_id(1)
    n_blocks = SPAN // BLK

    qi = lax.broadcasted_iota(jnp.int32, (BLK, 2 * BLK), 0)
    ki = lax.broadcasted_iota(jnp.int32, (BLK, 2 * BLK), 1)
    dist = qi + BLK - ki
    band = (dist >= 0) & (dist <= BLK)
    dist_f = dist.astype(F32)
    low_lanes = lax.broadcasted_iota(jnp.int32, (BLK, LANES), 1) < HEAD_DIM

    for g, (_, dil) in enumerate(GROUPS):
        nb = n_blocks // dil
        q_ref, kv_ref, kvp_ref = q_refs[g], kv_refs[g], kvp_refs[g]
        for r in range(dil):
            for jj in range(nb):
                bi = r * nb + jj
                kv_cols = slice(r * 2 * D_GROUP, (r + 1) * 2 * D_GROUP)
                qs_ref[bi] = q_ref[jj * BLK : (jj + 1) * BLK, r * D_GROUP : (r + 1) * D_GROUP]
                if jj == 0:
                    kvs_ref[bi, 0:BLK] = kvp_ref[:, kv_cols]
                else:
                    kvs_ref[bi, 0:BLK] = kv_ref[(jj - 1) * BLK : jj * BLK, kv_cols]
                kvs_ref[bi, BLK : 2 * BLK] = kv_ref[jj * BLK : (jj + 1) * BLK, kv_cols]

        def block_body(bi, carry, g=g, dil=dil, nb=nb):
            jj = bi % nb
            r = bi // nb
            is_first = jnp.logical_and(n == 0, jj == 0).astype(jnp.int32)
            valid = band & (ki >= BLK * is_first)
            row0 = jj * (BLK * dil) + r
            if dil == 1:
                rows = pl.ds(pl.multiple_of(row0, BLK), BLK)
            else:
                rows = pl.ds(row0, BLK, stride=dil)
            for hp in range(2):
                lanes = slice(hp * LANES, (hp + 1) * LANES)
                q2 = qs_ref[bi, :, lanes]
                k2 = kvs_ref[bi, :, lanes]
                v2 = kvs_ref[bi, :, D_GROUP + hp * LANES : D_GROUP + (hp + 1) * LANES]
                outs, lses = [], []
                for hh in range(2):
                    head_lanes = low_lanes if hh == 0 else jnp.logical_not(low_lanes)
                    qm = jnp.where(head_lanes, q2, jnp.zeros_like(q2))
                    s = _dot_nt(qm, k2)
                    slope = slopes_ref[g * HEADS_PER_GROUP + hp * 2 + hh] * float(dil)
                    s = jnp.where(valid, s - dist_f * slope, NEG_INF)
                    m = jnp.max(s, axis=-1, keepdims=True)
                    p = jnp.exp(s - m)
                    l = jnp.sum(p, axis=-1, keepdims=True)
                    pv = _dot(p.astype(BF16), v2)
                    outs.append(pv / l)
                    lses.append(jnp.broadcast_to(m + jnp.log(l), (BLK, LANES)))
                oscr_ref[g, hp, rows, :] = jnp.where(low_lanes, outs[0], outs[1])
                lscr_ref[g, hp, rows, :] = jnp.where(low_lanes, lses[0], lses[1])
            return carry

        lax.fori_loop(0, n_blocks, block_body, 0, unroll=ATTN_UNROLL)

    chunk = 256

    def combine_body(c, carry):
        rows = pl.ds(pl.multiple_of(c * chunk, chunk), chunk)
        for hp in range(2):
            l0 = lscr_ref[0, hp, rows, :]
            l1 = lscr_ref[1, hp, rows, :]
            l2 = lscr_ref[2, hp, rows, :]
            m = jnp.maximum(jnp.maximum(l0, l1), l2)
            w0 = jnp.exp(l0 - m)
            w1 = jnp.exp(l1 - m)
            w2 = jnp.exp(l2 - m)
            num = w0 * oscr_ref[0, hp, rows, :] + w1 * oscr_ref[1, hp, rows, :] + w2 * oscr_ref[2, hp, rows, :]
            o_ref[rows, hp * LANES : (hp + 1) * LANES] = (num / (w0 + w1 + w2)).astype(o_ref.dtype)
        return carry

    lax.fori_loop(0, SPAN // chunk, combine_body, 0)


def _prompt_attention(slopes, q_views, kv_views):
    bsz = q_views[0].shape[0]
    seq = q_views[0].shape[1] * GROUPS[0][1]
    assert seq % SPAN == 0
    n_blocks = SPAN // BLK
    q_specs, kv_specs, kvp_specs = [], [], []
    for _, dil in GROUPS:
        q_specs.append(pl.BlockSpec((None, SPAN // dil, dil * D_GROUP), lambda b, n: (b, n, 0)))
        kv_specs.append(pl.BlockSpec((None, SPAN // dil, dil * 2 * D_GROUP), lambda b, n: (b, n, 0)))
        per_span = n_blocks // dil
        kvp_specs.append(
            pl.BlockSpec(
                (None, BLK, dil * 2 * D_GROUP),
                lambda b, n, per_span=per_span: (b, jnp.maximum(n * per_span - 1, 0), 0),
            )
        )
    return pl.pallas_call(
        _prompt_attn_kernel,
        grid=(bsz, seq // SPAN),
        in_specs=[pl.BlockSpec(memory_space=pltpu.SMEM)] + q_specs + kv_specs + kvp_specs,
        out_specs=pl.BlockSpec((None, SPAN, D_GROUP), lambda b, n: (b, n, 0)),
        out_shape=jax.ShapeDtypeStruct((bsz, seq, D_GROUP), BF16),
        scratch_shapes=[
            pltpu.VMEM((n_blocks, BLK, D_GROUP), BF16),
            pltpu.VMEM((n_blocks, 2 * BLK, 2 * D_GROUP), BF16),
            pltpu.VMEM((N_GROUPS, 2, SPAN, LANES), F32),
            pltpu.VMEM((N_GROUPS, 2, SPAN, LANES), F32),
        ],
        compiler_params=pltpu.CompilerParams(
            dimension_semantics=("arbitrary", "arbitrary"), vmem_limit_bytes=VMEM_LIMIT_BYTES
        ),
        name="prompt_attention",
    )(slopes, *q_views, *kv_views, *kv_views)


Q_ROWS = 8


def _decode_attn_kernel(slopes_ref, q_ref, kn0, kn1, kn2, c0, c1, c2, o_ref, nc0, nc1, nc2, pad_ref, padt_ref, *,
                        t_new):
    kn_refs = (kn0, kn1, kn2)
    cache_refs = (c0, c1, c2)
    new_cache_refs = (nc0, nc1, nc2)
    n_rows = HEADS_PER_GROUP * Q_ROWS
    q_shift, head_shift = Q_ROWS.bit_length() - 1, HEAD_DIM.bit_length() - 1
    row = lax.broadcasted_iota(jnp.int32, (n_rows, D_GROUP), 0)
    lane = lax.broadcasted_iota(jnp.int32, (n_rows, D_GROUP), 1)
    head_sel = (row >> q_shift) == (lane >> head_shift)
    keep_shifted = lax.broadcasted_iota(jnp.int32, (D_GROUP, LANES), 1) < LANES - t_new

    pad_ref[...] = jnp.zeros_like(pad_ref)
    pad_ref[0:t_new] = q_ref[...]
    q_all = pad_ref[...]
    o_groups, lse_groups = [], []
    for g, (window, dil) in enumerate(GROUPS):
        cache, new_cache = cache_refs[g], new_cache_refs[g]
        padt_ref[...] = jnp.zeros_like(padt_ref)
        padt_ref[0:t_new] = kn_refs[g][...]
        new_t = padt_ref[...].T
        kv_new_t = (new_t[:D_GROUP], new_t[D_GROUP:])
        k_ext = jnp.concatenate([cache[0], kv_new_t[0]], axis=1).astype(BF16)
        v_ext = jnp.concatenate([cache[1], kv_new_t[1]], axis=1).astype(BF16)
        n_keys = window + LANES
        q_g = q_all[:, g * D_GROUP : (g + 1) * D_GROUP]
        q_exp = jnp.where(head_sel, jnp.concatenate([q_g] * HEADS_PER_GROUP, axis=0), 0.0).astype(BF16)
        s = _dot(q_exp, k_ext)
        srow = lax.broadcasted_iota(jnp.int32, (n_rows, n_keys), 0)
        col = lax.broadcasted_iota(jnp.int32, (n_rows, n_keys), 1)
        d_tok = window + (srow & (Q_ROWS - 1)) - col
        valid = (col < window + t_new) & (d_tok >= 0) & ((d_tok & (dil - 1)) == 0) & (d_tok <= window)
        slope = jnp.zeros((n_rows, 1), F32)
        hrow = lax.broadcasted_iota(jnp.int32, (n_rows, 1), 0) >> q_shift
        for hh in range(HEADS_PER_GROUP):
            slope = jnp.where(hrow == hh, slopes_ref[g * HEADS_PER_GROUP + hh], slope)
        s = jnp.where(valid, s - slope * d_tok.astype(F32), NEG_INF)
        m = jnp.max(s, axis=-1, keepdims=True)
        p = jnp.exp(s - m)
        l = jnp.sum(p, axis=-1, keepdims=True)
        o_groups.append(_dot_nt(p.astype(BF16), v_ext) / l)
        lse_groups.append(m + jnp.log(l))
        shift = LANES - t_new
        for kv in range(2):
            nxt = pltpu.roll(kv_new_t[kv], shift, axis=1)
            for j in reversed(range(window // LANES)):
                cur = pltpu.roll(cache[kv, :, j * LANES : (j + 1) * LANES], shift, axis=1)
                new_cache[kv, :, j * LANES : (j + 1) * LANES] = jnp.where(keep_shifted, cur, nxt)
                nxt = cur
    m = jnp.maximum(jnp.maximum(lse_groups[0], lse_groups[1]), lse_groups[2])
    ws = [jnp.exp(x - m) for x in lse_groups]
    o = (ws[0] * o_groups[0] + ws[1] * o_groups[1] + ws[2] * o_groups[2]) / (ws[0] + ws[1] + ws[2])
    o = jnp.where(head_sel, o, 0.0)
    o8 = o[0:Q_ROWS]
    for hh in range(1, HEADS_PER_GROUP):
        o8 = o8 + o[hh * Q_ROWS : (hh + 1) * Q_ROWS]
    o_ref[...] = o8[0:t_new]


def _decode_attention(slopes, q, kv_new, caches_t):
    bd, t_new, _ = q.shape
    assert t_new <= Q_ROWS
    cache_specs = []
    for g, (window, dil) in enumerate(GROUPS):
        assert caches_t[g].shape == (bd, 2, D_GROUP, window) and window % LANES == 0
        assert dil == 1 or dil >= t_new
        cache_specs.append(pl.BlockSpec((None, 2, D_GROUP, window), lambda b: (b, 0, 0, 0)))
    new_spec = pl.BlockSpec((None, t_new, 2 * D_GROUP), lambda b: (b, 0, 0))
    return pl.pallas_call(
        functools.partial(_decode_attn_kernel, t_new=t_new),
        grid=(bd,),
        in_specs=[pl.BlockSpec(memory_space=pltpu.SMEM), pl.BlockSpec((None, t_new, D_ATT), lambda b: (b, 0, 0))]
        + [new_spec] * N_GROUPS
        + cache_specs,
        out_specs=[pl.BlockSpec((None, t_new, D_GROUP), lambda b: (b, 0, 0))] + cache_specs,
        out_shape=[jax.ShapeDtypeStruct((bd, t_new, D_GROUP), F32)]
        + [jax.ShapeDtypeStruct(c.shape, F32) for c in caches_t],
        scratch_shapes=[pltpu.VMEM((Q_ROWS, D_ATT), F32), pltpu.VMEM((LANES, 2 * D_GROUP), F32)],
        compiler_params=pltpu.CompilerParams(
            dimension_semantics=("arbitrary",), vmem_limit_bytes=VMEM_LIMIT_BYTES
        ),
        name="decode_attention",
    )(slopes, q, *kv_new, *caches_t)


def _decode_conv_kernel(state_ref, u_ref, wdw_ref, bdw_ref, c_ref, ns_ref, xin_ref, *, t_new):
    n_state = CONV_WIDTH - 1
    xin_ref[0:n_state] = state_ref[...]
    xin_ref[n_state:] = u_ref[...]
    for t in range(t_new):
        acc = jnp.broadcast_to(bdw_ref[...], c_ref.shape[1:])
        for w in range(CONV_WIDTH):
            acc = acc + xin_ref[t + w] * wdw_ref[w : w + 1, :]
        c_ref[t] = acc
    ns_ref[...] = xin_ref[t_new:]


def _decode_conv(state_t, u_t, w_dw, b_dw):
    t_new, bd, ch = u_t.shape
    return pl.pallas_call(
        functools.partial(_decode_conv_kernel, t_new=t_new),
        in_specs=[_vmem_spec()] * 4,
        out_specs=[_vmem_spec()] * 2,
        out_shape=[jax.ShapeDtypeStruct((t_new, bd, ch), F32), jax.ShapeDtypeStruct(state_t.shape, F32)],
        scratch_shapes=[pltpu.VMEM((CONV_WIDTH - 1 + t_new, bd, ch), F32)],
        name="decode_conv",
    )(state_t, u_t, w_dw, b_dw)


CONV_ROWS = 128
N_CONV_SLABS = D_CONV // LANES


def _post_kernel(*refs, tm, conv_in_kernel):
    if conv_in_kernel:
        x_ref, u_ref, uprev_ref, oatt_ref, p_ref = refs[0:5]
        refs = refs[5:]
    else:
        x_ref, c_ref, oatt_ref, p_ref = refs[0:4]
        refs = refs[4:]
    (gmix_ref, wdw_ref, bdw_ref, lng_ref, lnb_ref, wco_ref, wao_ref, wgc_ref, wga_ref, wout_ref, gffn_ref,
     wfg_ref, wfu_ref, wfd_ref, gple_ref, wpg_ref, wple_ref, gfin_ref, y_ref, xin_ref, act_ref, acc_ref) = refs

    x = x_ref[...]
    h = _rms(x, gmix_ref[...]).astype(BF16)

    def ln_swish(slabs):
        mu = sum(jnp.sum(c, axis=-1, keepdims=True) for c in slabs) * (1.0 / D_CONV)
        cen = [c - mu for c in slabs]
        var = sum(jnp.sum(c * c, axis=-1, keepdims=True) for c in cen) * (1.0 / D_CONV)
        inv = lax.rsqrt(var + EPS)
        out = []
        for s, c in enumerate(cen):
            lanes = slice(s * LANES, (s + 1) * LANES)
            y = c * inv * lng_ref[:, lanes] + lnb_ref[:, lanes]
            out.append(y * jax.nn.sigmoid(y))
        return out

    if conv_in_kernel:
        i = pl.program_id(1)
        prev = uprev_ref[...]
        prev = jnp.where(i > 0, prev, jnp.zeros_like(prev))
        u = u_ref[...]
        for s in range(N_CONV_SLABS):
            xin_ref[s, 0:HALO_ROWS] = prev[:, s * LANES : (s + 1) * LANES]
            xin_ref[s, HALO_ROWS:] = u[:, s * LANES : (s + 1) * LANES]
        off = HALO_ROWS - (CONV_WIDTH - 1)
        half = CONV_ROWS // 2
        for rb in range(tm // CONV_ROWS):
            for par in range(2):
                accs = [jnp.broadcast_to(bdw_ref[:, s * LANES : (s + 1) * LANES], (half, LANES))
                        for s in range(N_CONV_SLABS)]
                for w in range(CONV_WIDTH):
                    rows = pl.ds(off + w + rb * CONV_ROWS + par, half, stride=2)
                    for s in range(N_CONV_SLABS):
                        accs[s] = accs[s] + xin_ref[s, rows, :] * wdw_ref[w : w + 1, s * LANES : (s + 1) * LANES]
                outs = ln_swish(accs)
                for s in range(N_CONV_SLABS):
                    act_ref[s, pl.ds(rb * CONV_ROWS + par, half, stride=2), :] = outs[s]
    else:
        c = c_ref[...]
        outs = ln_swish([c[:, s * LANES : (s + 1) * LANES] for s in range(N_CONV_SLABS)])
        for s in range(N_CONV_SLABS):
            act_ref[s] = outs[s]

    act = jnp.concatenate([act_ref[s] for s in range(N_CONV_SLABS)], axis=1).astype(BF16)
    oatt = oatt_ref[...].astype(BF16)
    y_conv = _dot(act, wco_ref[...])
    y_att = _dot(oatt, wao_ref[...])
    gate_c = jax.nn.sigmoid(_dot(h, wgc_ref[...]))
    gate_a = jax.nn.sigmoid(_dot(h, wga_ref[...]))
    x1 = x + _dot((gate_c * y_conv + gate_a * y_att).astype(BF16), wout_ref[...])

    hf = _rms(x1, gffn_ref[...]).astype(BF16)
    for c in range(D_FF // FF_CHUNK):
        gate = _dot(hf, wfg_ref[c])
        up = _dot(hf, wfu_ref[c])
        hidden = (gate * jax.nn.sigmoid(gate) * up).astype(BF16)
        if c == 0:
            acc_ref[...] = _dot(hidden, wfd_ref[c])
        else:
            acc_ref[...] += _dot(hidden, wfd_ref[c])
    x2 = x1 + acc_ref[...]

    hp = _rms(x2, gple_ref[...]).astype(BF16)
    x3 = x2 + jax.nn.sigmoid(_dot(hp, wpg_ref[...])) * _dot(p_ref[...].astype(BF16), wple_ref[...])
    y_ref[...] = _rms(x3, gfin_ref[...])


def _post(x, conv_in, oatt, p, weights, *, tm, conv_in_kernel):
    bsz, seq, _ = x.shape
    assert seq % tm == 0 and tm % CONV_ROWS == 0 and tm % HALO_ROWS == 0

    def row_spec(cols):
        return pl.BlockSpec((None, tm, cols), lambda b, i: (b, i, 0))

    data = [x, conv_in]
    data_specs = [row_spec(D_MODEL), row_spec(D_CONV)]
    if conv_in_kernel:
        per_tile = tm // HALO_ROWS
        data.append(conv_in)
        data_specs.append(
            pl.BlockSpec((None, HALO_ROWS, D_CONV), lambda b, i: (b, jnp.maximum(i * per_tile - 1, 0), 0))
        )
    data += [oatt, p]
    data_specs += [row_spec(D_GROUP), row_spec(D_PLE)]
    return pl.pallas_call(
        functools.partial(_post_kernel, tm=tm, conv_in_kernel=conv_in_kernel),
        grid=(bsz, seq // tm),
        in_specs=data_specs + [_vmem_spec()] * len(weights),
        out_specs=row_spec(D_MODEL),
        out_shape=jax.ShapeDtypeStruct((bsz, seq, D_MODEL), F32),
        scratch_shapes=[
            pltpu.VMEM((N_CONV_SLABS, tm + HALO_ROWS, LANES), F32),
            pltpu.VMEM((N_CONV_SLABS, tm, LANES), F32),
            pltpu.VMEM((tm, D_MODEL), F32),
        ],
        compiler_params=pltpu.CompilerParams(
            dimension_semantics=("arbitrary", "arbitrary"), vmem_limit_bytes=VMEM_LIMIT_BYTES
        ),
        name="post_prompt" if conv_in_kernel else "post_decode",
    )(*data, *weights)


def _layer(i, x_prompt, x_sample, state_conv, caches, p_prompt, p_sample, g_mix, w_in, w_dw, b_dw, ln_g, ln_b,
           w_conv_out, w_att_out, w_out, g_ffn, w_ffn_gate, w_ffn_up, w_ffn_down, g_ple, w_ple_gate, w_ple,
           g_final):
    bsz, seq, _ = x_prompt.shape
    bd, t_new, _ = x_sample.shape

    def row(v):
        return v.reshape(1, -1)

    o_q = 2 * D_CONV
    o_k = o_q + D_ATT
    o_v = o_k + D_ATT
    o_g = o_v + D_ATT
    w_in_b = w_in[i].astype(BF16)
    wglu, wq, wk, wv = w_in_b[:, :o_q], w_in_b[:, o_q:o_k], w_in_b[:, o_k:o_v], w_in_b[:, o_v:o_g]
    wgc, wga = w_in_b[:, o_g : o_g + D_MODEL], w_in_b[:, o_g + D_MODEL :]
    n_ff = D_FF // FF_CHUNK
    wfg = w_ffn_gate[i].astype(BF16).reshape(D_MODEL, n_ff, FF_CHUNK).transpose(1, 0, 2)
    wfu = w_ffn_up[i].astype(BF16).reshape(D_MODEL, n_ff, FF_CHUNK).transpose(1, 0, 2)
    wfd = w_ffn_down[i].astype(BF16).reshape(n_ff, FF_CHUNK, D_MODEL)
    post_weights = [
        row(g_mix[i]), w_dw[i], row(b_dw[i]), row(ln_g[i]), row(ln_b[i]), w_conv_out[i].astype(BF16),
        w_att_out[i].astype(BF16), wgc, wga, w_out[i].astype(BF16), row(g_ffn[i]), wfg, wfu, wfd, row(g_ple[i]),
        w_ple_gate[i].astype(BF16), w_ple[i].astype(BF16), row(g_final),
    ]
    slopes = 2.0 ** (-8.0 * (jnp.arange(N_ATT_HEADS, dtype=F32) + 1.0) / N_ATT_HEADS)

    outs = _inproj(x_prompt, row(g_mix[i]), wglu, wq, wk, wv, tm=512, prompt=True)
    u_p, q_p, kv_p, tails_p = outs[0], outs[1:4], outs[4:7], outs[7:10]
    o_p = _prompt_attention(slopes, q_p, kv_p)
    y_p = _post(x_prompt, u_p, o_p, p_prompt[i], post_weights, tm=256, conv_in_kernel=True)
    conv_p = u_p[:, seq - (CONV_WIDTH - 1) :]
    kv_caches_p = [
        t.reshape(bsz, 2, HEADS_PER_GROUP, HEAD_DIM, t.shape[-1]).transpose(0, 4, 1, 2, 3) for t in tails_p
    ]

    n_s = bd * t_new
    outs = _inproj(x_sample.reshape(1, n_s, D_MODEL), row(g_mix[i]), wglu, wq, wk, wv, tm=n_s, prompt=False)
    u_s = outs[0].reshape(bd, t_new, D_CONV)
    q_s = jnp.concatenate(outs[1:4], axis=-1).reshape(bd, t_new, D_ATT)
    kv_s = [t.reshape(bd, t_new, 2 * D_GROUP) for t in outs[4:7]]
    caches_t = [c[i].transpose(0, 2, 3, 4, 1).reshape(bd, 2, D_GROUP, c.shape[2]) for c in caches]
    outs = _decode_attention(slopes, q_s, kv_s, caches_t)
    o_s, new_caches_t = outs[0], outs[1:]
    c_s, new_state_t = _decode_conv(state_conv[i].transpose(1, 0, 2), u_s.transpose(1, 0, 2), w_dw[i], row(b_dw[i]))
    y_s = _post(
        x_sample.reshape(1, n_s, D_MODEL), c_s.transpose(1, 0, 2).reshape(1, n_s, D_CONV),
        o_s.reshape(1, n_s, D_GROUP), p_sample[i].reshape(1, n_s, D_PLE), post_weights, tm=n_s,
        conv_in_kernel=False,
    ).reshape(bd, t_new, D_MODEL)
    conv_s = new_state_t.transpose(1, 0, 2)
    kv_caches_s = [
        t.reshape(bd, 2, HEADS_PER_GROUP, HEAD_DIM, t.shape[-1]).transpose(0, 4, 1, 2, 3) for t in new_caches_t
    ]
    return y_p, y_s, conv_p, conv_s, kv_caches_p, kv_caches_s


def kernel(x_prompt, x_sample, state_conv, cache_kv_g0, cache_kv_g1, cache_kv_g2, p_prompt, p_sample, g_mix, w_in,
           w_dw, b_dw, ln_g, ln_b, w_conv_out, w_att_out, w_out, g_ffn, w_ffn_gate, w_ffn_up, w_ffn_down, g_ple,
           w_ple_gate, w_ple, g_final):
    depth = w_in.shape[0]
    assert depth == 1
    y_p, y_s, conv_p, conv_s, kv_p, kv_s = _layer(
        0, x_prompt, x_sample, state_conv, (cache_kv_g0, cache_kv_g1, cache_kv_g2), p_prompt, p_sample, g_mix, w_in,
        w_dw, b_dw, ln_g, ln_b, w_conv_out, w_att_out, w_out, g_ffn, w_ffn_gate, w_ffn_up, w_ffn_down, g_ple,
        w_ple_gate, w_ple, g_final,
    )
    return (y_p, y_s, conv_p[None], conv_s[None], kv_p[0][None], kv_s[0][None], kv_p[1][None], kv_s[1][None],
            kv_p[2][None], kv_s[2][None])
```

```python
import functools

import jax
import jax.numpy as jnp
from jax import lax
from jax.experimental import pallas as pl
from jax.experimental.pallas import tpu as pltpu

D_MODEL = 1024
D_CONV = 512
CONV_WIDTH = 31
HEAD_DIM = 64
HEADS_PER_GROUP = 4
GROUPS = ((128, 1), (512, 4), (2048, 16))
N_GROUPS = len(GROUPS)
N_ATT_HEADS = HEADS_PER_GROUP * N_GROUPS
D_GROUP = HEADS_PER_GROUP * HEAD_DIM
D_ATT = N_ATT_HEADS * HEAD_DIM
D_FF = 2816
D_PLE = 256
BLK = 128
EPS = 1e-6
NEG_INF = -1e30

LANES = 128
N_SLAB = D_GROUP // LANES
SPAN = 2048
FF_CHUNK = 256
HALO_ROWS = 32
VMEM_LIMIT_BYTES = 56 * 1024 * 1024

BF16 = jnp.bfloat16
F32 = jnp.float32


def _rms(x, g):
    return x * lax.rsqrt(jnp.mean(x * x, axis=-1, keepdims=True) + EPS) * g


def _dot(a, b):
    return jnp.dot(a, b, preferred_element_type=F32)


def _dot_nt(a, b):
    return lax.dot_general(a, b, (((1,), (1,)), ((), ())), preferred_element_type=F32)


def _vmem_spec():
    return pl.BlockSpec(memory_space=pltpu.VMEM)


def _inproj_kernel(x_ref, g_ref, wglu_ref, wq_ref, wk_ref, wv_ref, u_ref, *refs, tm, n_tiles, prompt):
    q_refs = refs[0:N_GROUPS]
    kv_refs = refs[N_GROUPS : 2 * N_GROUPS]
    tail_refs = refs[2 * N_GROUPS : 3 * N_GROUPS] if prompt else ()
    slab_ref = refs[-1] if prompt else None
    i = pl.program_id(1)
    h = _rms(x_ref[...], g_ref[...]).astype(BF16)
    a = _dot(h, wglu_ref[:, :D_CONV])
    b = _dot(h, wglu_ref[:, D_CONV:])
    u_ref[...] = a * jax.nn.sigmoid(b)
    for g, (window, dil) in enumerate(GROUPS):
        cols = slice(g * D_GROUP, (g + 1) * D_GROUP)
        q = _dot(h, wq_ref[:, cols]) * (HEAD_DIM**-0.5)
        k = _dot(h, wk_ref[:, cols])
        v = _dot(h, wv_ref[:, cols])
        q_ref, kv_ref = q_refs[g], kv_refs[g]
        if not prompt or dil == 1:
            q_ref[...] = q.astype(q_ref.dtype)
            kv_ref[:, :D_GROUP] = k.astype(kv_ref.dtype)
            kv_ref[:, D_GROUP:] = v.astype(kv_ref.dtype)
        if prompt:
            for j, val in enumerate((q, k, v)):
                if dil == 1 and j == 0:
                    continue
                for s in range(N_SLAB):
                    slab_ref[g, j * N_SLAB + s] = val[:, s * LANES : (s + 1) * LANES]
        if prompt and dil > 1:
            for r in range(dil):
                rows = pl.ds(r, tm // dil, stride=dil)
                for s in range(N_SLAB):
                    lane0 = s * LANES
                    q_ref[:, r * D_GROUP + lane0 : r * D_GROUP + lane0 + LANES] = slab_ref[g, s, rows, :].astype(
                        q_ref.dtype
                    )
                    for j in (1, 2):
                        c0 = r * 2 * D_GROUP + (j - 1) * D_GROUP + lane0
                        kv_ref[:, c0 : c0 + LANES] = slab_ref[g, j * N_SLAB + s, rows, :].astype(kv_ref.dtype)

    if prompt:
        def write_tail(g, rows):
            for kv in range(2):
                for s in range(N_SLAB):
                    tile = slab_ref[g, (1 + kv) * N_SLAB + s, rows, :]
                    tail_refs[g][kv, s * LANES : (s + 1) * LANES, :] = tile.T

        max_tiles = max(window // tm for window, _ in GROUPS)

        @pl.when(i >= n_tiles - max_tiles)
        def _():
            for g, (window, _) in enumerate(GROUPS):
                if window // tm == max_tiles:
                    write_tail(g, slice(None))

            @pl.when(i == n_tiles - 1)
            def _():
                for g, (window, _) in enumerate(GROUPS):
                    if window <= tm:
                        write_tail(g, slice(tm - window, tm))


def _inproj(x, g_mix, wglu, wq, wk, wv, *, tm, prompt):
    bsz, seq, _ = x.shape
    n_tiles = seq // tm
    assert seq % tm == 0
    qkv_dtype = BF16 if prompt else F32

    def row_spec(rows, cols):
        return pl.BlockSpec((None, rows, cols), lambda b, i: (b, i, 0))

    out_shape = [jax.ShapeDtypeStruct((bsz, seq, D_CONV), F32)]
    out_specs = [row_spec(tm, D_CONV)]
    dils = [dil if prompt else 1 for _, dil in GROUPS]
    for width in (D_GROUP, 2 * D_GROUP):
        for dil in dils:
            assert tm % (16 * dil) == 0
            out_shape.append(jax.ShapeDtypeStruct((bsz, seq // dil, dil * width), qkv_dtype))
            out_specs.append(row_spec(tm // dil, dil * width))
    scratch = []
    if prompt:
        scratch.append(pltpu.VMEM((N_GROUPS, 3 * N_SLAB, tm, LANES), F32))
        max_window = max(window for window, _ in GROUPS)
        for window, _ in GROUPS:
            assert window <= seq and (window <= tm or (window == max_window and window % tm == 0))
            out_shape.append(jax.ShapeDtypeStruct((bsz, 2, D_GROUP, window), F32))
            first = n_tiles - max(window // tm, 1)
            out_specs.append(
                pl.BlockSpec(
                    (None, 2, D_GROUP, min(window, tm)),
                    lambda b, i, first=first: (b, 0, 0, jnp.maximum(i - first, 0)),
                )
            )
    return pl.pallas_call(
        functools.partial(_inproj_kernel, tm=tm, n_tiles=n_tiles, prompt=prompt),
        grid=(bsz, n_tiles),
        in_specs=[row_spec(tm, D_MODEL)] + [_vmem_spec()] * 5,
        out_specs=out_specs,
        out_shape=out_shape,
        scratch_shapes=scratch,
        compiler_params=pltpu.CompilerParams(
            dimension_semantics=("arbitrary", "arbitrary"), vmem_limit_bytes=VMEM_LIMIT_BYTES
        ),
        name="inproj_prompt" if prompt else "inproj_decode",
    )(x, g_mix, wglu, wq, wk, wv)


ATTN_UNROLL = 4


def _prompt_attn_kernel(slopes_ref, *refs):
    q_refs = refs[0:3]
    kv_refs = refs[3:6]
    kvp_refs = refs[6:9]
    o_ref = refs[9]
    qs_ref, kvs_ref, oscr_ref, lscr_ref = refs[10:14]
    n = pl.program_id(1)
    n_blocks = SPAN // BLK

    qi = lax.broadcasted_iota(jnp.int32, (BLK, 2 * BLK), 0)
    ki = lax.broadcasted_iota(jnp.int32, (BLK, 2 * BLK), 1)
    dist = qi + BLK - ki
    band = (dist >= 0) & (dist <= BLK)
    dist_f = dist.astype(F32)
    low_lanes = lax.broadcasted_iota(jnp.int32, (BLK, LANES), 1) < HEAD_DIM

    for g, (_, dil) in enumerate(GROUPS):
        nb = n_blocks // dil
        q_ref, kv_ref, kvp_ref = q_refs[g], kv_refs[g], kvp_refs[g]
        for r in range(dil):
            for jj in range(nb):
                bi = r * nb + jj
                kv_cols = slice(r * 2 * D_GROUP, (r + 1) * 2 * D_GROUP)
                qs_ref[bi] = q_ref[jj * BLK : (jj + 1) * BLK, r * D_GROUP : (r + 1) * D_GROUP]
                if jj == 0:
                    kvs_ref[bi, 0:BLK] = kvp_ref[:, kv_cols]
                else:
                    kvs_ref[bi, 0:BLK] = kv_ref[(jj - 1) * BLK : jj * BLK, kv_cols]
                kvs_ref[bi, BLK : 2 * BLK] = kv_ref[jj * BLK : (jj + 1) * BLK, kv_cols]

        def block_body(bi, carry, g=g, dil=dil, nb=nb):
            jj = bi % nb
            r = bi // nb
            is_first = jnp.logical_and(n == 0, jj == 0).astype(jnp.int32)
            valid = band & (ki >= BLK * is_first)
            row0 = jj * (BLK * dil) + r
            if dil == 1:
                rows = pl.ds(pl.multiple_of(row0, BLK), BLK)
            else:
                rows = pl.ds(row0, BLK, stride=dil)
            for hp in range(2):
                lanes = slice(hp * LANES, (hp + 1) * LANES)
                q2 = qs_ref[bi, :, lanes]
                k2 = kvs_ref[bi, :, lanes]
                v2 = kvs_ref[bi, :, D_GROUP + hp * LANES : D_GROUP + (hp + 1) * LANES]
                outs, lses = [], []
                for hh in range(2):
                    head_lanes = low_lanes if hh == 0 else jnp.logical_not(low_lanes)
                    qm = jnp.where(head_lanes, q2, jnp.zeros_like(q2))
                    s = _dot_nt(qm, k2)
                    slope = slopes_ref[g * HEADS_PER_GROUP + hp * 2 + hh] * float(dil)
                    s = jnp.where(valid, s - dist_f * slope, NEG_INF)
                    m = jnp.max(s, axis=-1, keepdims=True)
                    p = jnp.exp(s - m)
                    l = jnp.sum(p, axis=-1, keepdims=True)
                    pv = _dot(p.astype(BF16), v2)
                    outs.append(pv / l)
                    lses.append(jnp.broadcast_to(m + jnp.log(l), (BLK, LANES)))
                oscr_ref[g, hp, rows, :] = jnp.where(low_lanes, outs[0], outs[1])
                lscr_ref[g, hp, rows, :] = jnp.where(low_lanes, lses[0], lses[1])
            return carry

        lax.fori_loop(0, n_blocks, block_body, 0, unroll=ATTN_UNROLL)

    chunk = 256

    def combine_body(c, carry):
        rows = pl.ds(pl.multiple_of(c * chunk, chunk), chunk)
        for hp in range(2):
            l0 = lscr_ref[0, hp, rows, :]
            l1 = lscr_ref[1, hp, rows, :]
            l2 = lscr_ref[2, hp, rows, :]
            m = jnp.maximum(jnp.maximum(l0, l1), l2)
            w0 = jnp.exp(l0 - m)
            w1 = jnp.exp(l1 - m)
            w2 = jnp.exp(l2 - m)
            num = w0 * oscr_ref[0, hp, rows, :] + w1 * oscr_ref[1, hp, rows, :] + w2 * oscr_ref[2, hp, rows, :]
            o_ref[rows, hp * LANES : (hp + 1) * LANES] = (num / (w0 + w1 + w2)).astype(o_ref.dtype)
        return carry

    lax.fori_loop(0, SPAN // chunk, combine_body, 0)


def _prompt_attention(slopes, q_views, kv_views):
    bsz = q_views[0].shape[0]
    seq = q_views[0].shape[1] * GROUPS[0][1]
    assert seq % SPAN == 0
    n_blocks = SPAN // BLK
    q_specs, kv_specs, kvp_specs = [], [], []
    for _, dil in GROUPS:
        q_specs.append(pl.BlockSpec((None, SPAN // dil, dil * D_GROUP), lambda b, n: (b, n, 0)))
        kv_specs.append(pl.BlockSpec((None, SPAN // dil, dil * 2 * D_GROUP), lambda b, n: (b, n, 0)))
        per_span = n_blocks // dil
        kvp_specs.append(
            pl.BlockSpec(
                (None, BLK, dil * 2 * D_GROUP),
                lambda b, n, per_span=per_span: (b, jnp.maximum(n * per_span - 1, 0), 0),
            )
        )
    return pl.pallas_call(
        _prompt_attn_kernel,
        grid=(bsz, seq // SPAN),
        in_specs=[pl.BlockSpec(memory_space=pltpu.SMEM)] + q_specs + kv_specs + kvp_specs,
        out_specs=pl.BlockSpec((None, SPAN, D_GROUP), lambda b, n: (b, n, 0)),
        out_shape=jax.ShapeDtypeStruct((bsz, seq, D_GROUP), BF16),
        scratch_shapes=[
            pltpu.VMEM((n_blocks, BLK, D_GROUP), BF16),
            pltpu.VMEM((n_blocks, 2 * BLK, 2 * D_GROUP), BF16),
            pltpu.VMEM((N_GROUPS, 2, SPAN, LANES), F32),
            pltpu.VMEM((N_GROUPS, 2, SPAN, LANES), F32),
        ],
        compiler_params=pltpu.CompilerParams(
            dimension_semantics=("arbitrary", "arbitrary"), vmem_limit_bytes=VMEM_LIMIT_BYTES
        ),
        name="prompt_attention",
    )(slopes, *q_views, *kv_views, *kv_views)


Q_ROWS = 8


def _decode_attn_kernel(slopes_ref, q_ref, kn0, kn1, kn2, c0, c1, c2, o_ref, nc0, nc1, nc2, pad_ref, padt_ref, *,
                        t_new):
    kn_refs = (kn0, kn1, kn2)
    cache_refs = (c0, c1, c2)
    new_cache_refs = (nc0, nc1, nc2)
    n_rows = HEADS_PER_GROUP * Q_ROWS
    q_shift, head_shift = Q_ROWS.bit_length() - 1, HEAD_DIM.bit_length() - 1
    row = lax.broadcasted_iota(jnp.int32, (n_rows, D_GROUP), 0)
    lane = lax.broadcasted_iota(jnp.int32, (n_rows, D_GROUP), 1)
    head_sel = (row >> q_shift) == (lane >> head_shift)
    keep_shifted = lax.broadcasted_iota(jnp.int32, (D_GROUP, LANES), 1) < LANES - t_new

    pad_ref[...] = jnp.zeros_like(pad_ref)
    pad_ref[0:t_new] = q_ref[...]
    q_all = pad_ref[...]
    o_groups, lse_groups = [], []
    for g, (window, dil) in enumerate(GROUPS):
        cache, new_cache = cache_refs[g], new_cache_refs[g]
        padt_ref[...] = jnp.zeros_like(padt_ref)
        padt_ref[0:t_new] = kn_refs[g][...]
        new_t = padt_ref[...].T
        kv_new_t = (new_t[:D_GROUP], new_t[D_GROUP:])
        k_ext = jnp.concatenate([cache[0], kv_new_t[0]], axis=1).astype(BF16)
        v_ext = jnp.concatenate([cache[1], kv_new_t[1]], axis=1).astype(BF16)
        n_keys = window + LANES
        q_g = q_all[:, g * D_GROUP : (g + 1) * D_GROUP]
        q_exp = jnp.where(head_sel, jnp.concatenate([q_g] * HEADS_PER_GROUP, axis=0), 0.0).astype(BF16)
        s = _dot(q_exp, k_ext)
        srow = lax.broadcasted_iota(jnp.int32, (n_rows, n_keys), 0)
        col = lax.broadcasted_iota(jnp.int32, (n_rows, n_keys), 1)
        d_tok = window + (srow & (Q_ROWS - 1)) - col
        valid = (col < window + t_new) & (d_tok >= 0) & ((d_tok & (dil - 1)) == 0) & (d_tok <= window)
        slope = jnp.zeros((n_rows, 1), F32)
        hrow = lax.broadcasted_iota(jnp.int32, (n_rows, 1), 0) >> q_shift
        for hh in range(HEADS_PER_GROUP):
            slope = jnp.where(hrow == hh, slopes_ref[g * HEADS_PER_GROUP + hh], slope)
        s = jnp.where(valid, s - slope * d_tok.astype(F32), NEG_INF)
        m = jnp.max(s, axis=-1, keepdims=True)
        p = jnp.exp(s - m)
        l = jnp.sum(p, axis=-1, keepdims=True)
        o_groups.append(_dot_nt(p.astype(BF16), v_ext) / l)
        lse_groups.append(m + jnp.log(l))
        shift = LANES - t_new
        for kv in range(2):
            nxt = pltpu.roll(kv_new_t[kv], shift, axis=1)
            for j in reversed(range(window // LANES)):
                cur = pltpu.roll(cache[kv, :, j * LANES : (j + 1) * LANES], shift, axis=1)
                new_cache[kv, :, j * LANES : (j + 1) * LANES] = jnp.where(keep_shifted, cur, nxt)
                nxt = cur
    m = jnp.maximum(jnp.maximum(lse_groups[0], lse_groups[1]), lse_groups[2])
    ws = [jnp.exp(x - m) for x in lse_groups]
    o = (ws[0] * o_groups[0] + ws[1] * o_groups[1] + ws[2] * o_groups[2]) / (ws[0] + ws[1] + ws[2])
    o = jnp.where(head_sel, o, 0.0)
    o8 = o[0:Q_ROWS]
    for hh in range(1, HEADS_PER_GROUP):
        o8 = o8 + o[hh * Q_ROWS : (hh + 1) * Q_ROWS]
    o_ref[...] = o8[0:t_new]


def _decode_attention(slopes, q, kv_new, caches_t):
    bd, t_new, _ = q.shape
    assert t_new <= Q_ROWS
    cache_specs = []
    for g, (window, dil) in enumerate(GROUPS):
        assert caches_t[g].shape == (bd, 2, D_GROUP, window) and window % LANES == 0
        assert dil == 1 or dil >= t_new
        cache_specs.append(pl.BlockSpec((None, 2, D_GROUP, window), lambda b: (b, 0, 0, 0)))
    new_spec = pl.BlockSpec((None, t_new, 2 * D_GROUP), lambda b: (b, 0, 0))
    return pl.pallas_call(
        functools.partial(_decode_attn_kernel, t_new=t_new),
        grid=(bd,),
        in_specs=[pl.BlockSpec(memory_space=pltpu.SMEM), pl.BlockSpec((None, t_new, D_ATT), lambda b: (b, 0, 0))]
        + [new_spec] * N_GROUPS
        + cache_specs,
        out_specs=[pl.BlockSpec((None, t_new, D_GROUP), lambda b: (b, 0, 0))] + cache_specs,
        out_shape=[jax.ShapeDtypeStruct((bd, t_new, D_GROUP), F32)]
        + [jax.ShapeDtypeStruct(c.shape, F32) for c in caches_t],
        scratch_shapes=[pltpu.VMEM((Q_ROWS, D_ATT), F32), pltpu.VMEM((LANES, 2 * D_GROUP), F32)],
        compiler_params=pltpu.CompilerParams(
            dimension_semantics=("arbitrary",), vmem_limit_bytes=VMEM_LIMIT_BYTES
        ),
        name="decode_attention",
    )(slopes, q, *kv_new, *caches_t)


def _decode_conv_kernel(state_ref, u_ref, wdw_ref, bdw_ref, c_ref, ns_ref, xin_ref, *, t_new):
    n_state = CONV_WIDTH - 1
    xin_ref[0:n_state] = state_ref[...]
    xin_ref[n_state:] = u_ref[...]
    for t in range(t_new):
        acc = jnp.broadcast_to(bdw_ref[...], c_ref.shape[1:])
        for w in range(CONV_WIDTH):
            acc = acc + xin_ref[t + w] * wdw_ref[w : w + 1, :]
        c_ref[t] = acc
    ns_ref[...] = xin_ref[t_new:]


def _decode_conv(state_t, u_t, w_dw, b_dw):
    t_new, bd, ch = u_t.shape
    return pl.pallas_call(
        functools.partial(_decode_conv_kernel, t_new=t_new),
        in_specs=[_vmem_spec()] * 4,
        out_specs=[_vmem_spec()] * 2,
        out_shape=[jax.ShapeDtypeStruct((t_new, bd, ch), F32), jax.ShapeDtypeStruct(state_t.shape, F32)],
        scratch_shapes=[pltpu.VMEM((CONV_WIDTH - 1 + t_new, bd, ch), F32)],
        name="decode_conv",
    )(state_t, u_t, w_dw, b_dw)


CONV_ROWS = 64
SUB_ROWS = 256
FFN_LEAD_CHUNKS = 3
N_CONV_SLABS = D_CONV // LANES


def _post_kernel(*refs, tm, conv_in_kernel):
    if conv_in_kernel:
        x_ref, u_ref, uhalo_ref, oatt_ref, p_ref = refs[0:5]
        refs = refs[5:]
    else:
        x_ref, c_ref, oatt_ref, p_ref = refs[0:4]
        refs = refs[4:]
    (gmix_ref, wdw_ref, bdw_ref, lng_ref, lnb_ref, wco_ref, wao_ref, wgc_ref, wga_ref, wout_ref, gffn_ref,
     wfg_ref, wfu_ref, wfd_ref, gple_ref, wpg_ref, wple_ref, gfin_ref, y_ref, xin_ref, act_ref, acc_ref) = refs

    sub = min(tm, SUB_ROWS)
    subs = [slice(j * sub, (j + 1) * sub) for j in range(tm // sub)]

    def ln_swish(slabs):
        mu = sum(jnp.sum(c, axis=-1, keepdims=True) for c in slabs) * (1.0 / D_CONV)
        cen = [c - mu for c in slabs]
        var = sum(jnp.sum(c * c, axis=-1, keepdims=True) for c in cen) * (1.0 / D_CONV)
        inv = lax.rsqrt(var + EPS)
        out = []
        for s, c in enumerate(cen):
            lanes = slice(s * LANES, (s + 1) * LANES)
            y = c * inv * lng_ref[:, lanes] + lnb_ref[:, lanes]
            out.append(y * jax.nn.sigmoid(y))
        return out

    if conv_in_kernel:
        i = pl.program_id(1)
        prev = uhalo_ref[...]
        prev = jnp.where(i > 0, prev, jnp.zeros_like(prev))
        u = u_ref[...]
        for s in range(N_CONV_SLABS):
            xin_ref[s, 0:HALO_ROWS] = prev[:, s * LANES : (s + 1) * LANES]
            xin_ref[s, HALO_ROWS:] = u[:, s * LANES : (s + 1) * LANES]

    def conv_branch(rows):
        if not conv_in_kernel:
            c = c_ref[rows]
            outs = ln_swish([c[:, s * LANES : (s + 1) * LANES] for s in range(N_CONV_SLABS)])
            return jnp.concatenate(outs, axis=1).astype(BF16)
        off = HALO_ROWS - (CONV_WIDTH - 1)
        half = CONV_ROWS // 2
        for r0 in range(rows.start, rows.stop, CONV_ROWS):
            for par in range(2):
                accs = [jnp.broadcast_to(bdw_ref[:, s * LANES : (s + 1) * LANES], (half, LANES))
                        for s in range(N_CONV_SLABS)]
                for w in range(CONV_WIDTH):
                    taps = pl.ds(off + w + r0 + par, half, stride=2)
                    for s in range(N_CONV_SLABS):
                        accs[s] = accs[s] + xin_ref[s, taps, :] * wdw_ref[w : w + 1, s * LANES : (s + 1) * LANES]
                outs = ln_swish(accs)
                for s in range(N_CONV_SLABS):
                    act_ref[s, pl.ds(r0 + par, half, stride=2), :] = outs[s]
        return jnp.concatenate([act_ref[s, rows] for s in range(N_CONV_SLABS)], axis=1).astype(BF16)

    x1s, ple_values, gated = [], [], []
    for rows in subs:
        h = _rms(x_ref[rows], gmix_ref[...]).astype(BF16)
        gate_c = jax.nn.sigmoid(_dot(h, wgc_ref[...]))
        gate_a = jax.nn.sigmoid(_dot(h, wga_ref[...]))
        gated.append((gate_c, gate_a * _dot(oatt_ref[rows].astype(BF16), wao_ref[...])))
        ple_values.append(_dot(p_ref[rows].astype(BF16), wple_ref[...]))
    for rows, (gate_c, gated_att) in zip(subs, gated):
        y_conv = _dot(conv_branch(rows), wco_ref[...])
        x1s.append(x_ref[rows] + _dot((gate_c * y_conv + gated_att).astype(BF16), wout_ref[...]))

    n_ff = D_FF // FF_CHUNK

    def ffn_start(x1):
        hf = _rms(x1, gffn_ref[...]).astype(BF16)
        return hf, (_dot(hf, wfg_ref[0]), _dot(hf, wfu_ref[0]))

    def ffn_chunks(rows, state, lo, hi):
        hf, gate_up = state
        for c in range(lo, hi):
            gate, up = gate_up
            if c + 1 < n_ff:
                gate_up = (_dot(hf, wfg_ref[c + 1]), _dot(hf, wfu_ref[c + 1]))
            hidden = (gate * jax.nn.sigmoid(gate) * up).astype(BF16)
            if c == 0:
                acc_ref[rows] = _dot(hidden, wfd_ref[c])
            else:
                acc_ref[rows] += _dot(hidden, wfd_ref[c])
        return hf, gate_up

    def finish(rows, x1, ple_value):
        x2 = x1 + acc_ref[rows]
        hp = _rms(x2, gple_ref[...]).astype(BF16)
        x3 = x2 + jax.nn.sigmoid(_dot(hp, wpg_ref[...])) * ple_value
        y_ref[rows] = _rms(x3, gfin_ref[...])

    ffn_chunks(subs[0], ffn_start(x1s[0]), 0, n_ff)
    for j in range(1, len(subs)):
        state = ffn_chunks(subs[j], ffn_start(x1s[j]), 0, FFN_LEAD_CHUNKS)
        finish(subs[j - 1], x1s[j - 1], ple_values[j - 1])
        ffn_chunks(subs[j], state, FFN_LEAD_CHUNKS, n_ff)
    finish(subs[-1], x1s[-1], ple_values[-1])


def _post(x, conv_in, oatt, p, weights, *, tm, conv_in_kernel):
    bsz, seq, _ = x.shape
    assert seq % tm == 0 and tm % CONV_ROWS == 0 and tm % HALO_ROWS == 0
    per_tile = tm // HALO_ROWS

    def row_spec(cols):
        return pl.BlockSpec((None, tm, cols), lambda b, i: (b, i, 0))

    data = [x, conv_in]
    data_specs = [row_spec(D_MODEL), row_spec(D_CONV)]
    if conv_in_kernel:
        data.append(conv_in)
        data_specs.append(
            pl.BlockSpec((None, HALO_ROWS, D_CONV), lambda b, i: (b, jnp.maximum(i * per_tile - 1, 0), 0))
        )
    data += [oatt, p]
    data_specs += [row_spec(D_GROUP), row_spec(D_PLE)]
    return pl.pallas_call(
        functools.partial(_post_kernel, tm=tm, conv_in_kernel=conv_in_kernel),
        grid=(bsz, seq // tm),
        in_specs=data_specs + [_vmem_spec()] * len(weights),
        out_specs=row_spec(D_MODEL),
        out_shape=jax.ShapeDtypeStruct((bsz, seq, D_MODEL), F32),
        scratch_shapes=[
            pltpu.VMEM((N_CONV_SLABS, tm + HALO_ROWS, LANES), F32),
            pltpu.VMEM((N_CONV_SLABS, tm, LANES), F32),
            pltpu.VMEM((tm, D_MODEL), F32),
        ],
        compiler_params=pltpu.CompilerParams(
            dimension_semantics=("arbitrary", "arbitrary"), vmem_limit_bytes=VMEM_LIMIT_BYTES
        ),
        name="post_prompt" if conv_in_kernel else "post_decode",
    )(*data, *weights)


def _layer(i, x_prompt, x_sample, state_conv, caches, p_prompt, p_sample, g_mix, w_in, w_dw, b_dw, ln_g, ln_b,
           w_conv_out, w_att_out, w_out, g_ffn, w_ffn_gate, w_ffn_up, w_ffn_down, g_ple, w_ple_gate, w_ple,
           g_final):
    bsz, seq, _ = x_prompt.shape
    bd, t_new, _ = x_sample.shape

    def row(v):
        return v.reshape(1, -1)

    o_q = 2 * D_CONV
    o_k = o_q + D_ATT
    o_v = o_k + D_ATT
    o_g = o_v + D_ATT
    w_in_b = w_in[i].astype(BF16)
    wglu, wq, wk, wv = w_in_b[:, :o_q], w_in_b[:, o_q:o_k], w_in_b[:, o_k:o_v], w_in_b[:, o_v:o_g]
    wgc, wga = w_in_b[:, o_g : o_g + D_MODEL], w_in_b[:, o_g + D_MODEL :]
    n_ff = D_FF // FF_CHUNK
    wfg = w_ffn_gate[i].astype(BF16).reshape(D_MODEL, n_ff, FF_CHUNK).transpose(1, 0, 2)
    wfu = w_ffn_up[i].astype(BF16).reshape(D_MODEL, n_ff, FF_CHUNK).transpose(1, 0, 2)
    wfd = w_ffn_down[i].astype(BF16).reshape(n_ff, FF_CHUNK, D_MODEL)
    post_weights = [
        row(g_mix[i]), w_dw[i], row(b_dw[i]), row(ln_g[i]), row(ln_b[i]), w_conv_out[i].astype(BF16),
        w_att_out[i].astype(BF16), wgc, wga, w_out[i].astype(BF16), row(g_ffn[i]), wfg, wfu, wfd, row(g_ple[i]),
        w_ple_gate[i].astype(BF16), w_ple[i].astype(BF16), row(g_final),
    ]
    slopes = 2.0 ** (-8.0 * (jnp.arange(N_ATT_HEADS, dtype=F32) + 1.0) / N_ATT_HEADS)

    outs = _inproj(x_prompt, row(g_mix[i]), wglu, wq, wk, wv, tm=512, prompt=True)
    u_p, q_p, kv_p, tails_p = outs[0], outs[1:4], outs[4:7], outs[7:10]
    o_p = _prompt_attention(slopes, q_p, kv_p)
    y_p = _post(x_prompt, u_p, o_p, p_prompt[i], post_weights, tm=512, conv_in_kernel=True)
    conv_p = u_p[:, seq - (CONV_WIDTH - 1) :]
    kv_caches_p = [
        t.reshape(bsz, 2, HEADS_PER_GROUP, HEAD_DIM, t.shape[-1]).transpose(0, 4, 1, 2, 3) for t in tails_p
    ]

    n_s = bd * t_new
    outs = _inproj(x_sample.reshape(1, n_s, D_MODEL), row(g_mix[i]), wglu, wq, wk, wv, tm=n_s, prompt=False)
    u_s = outs[0].reshape(bd, t_new, D_CONV)
    q_s = jnp.concatenate(outs[1:4], axis=-1).reshape(bd, t_new, D_ATT)
    kv_s = [t.reshape(bd, t_new, 2 * D_GROUP) for t in outs[4:7]]
    caches_t = [c[i].transpose(0, 2, 3, 4, 1).reshape(bd, 2, D_GROUP, c.shape[2]) for c in caches]
    outs = _decode_attention(slopes, q_s, kv_s, caches_t)
    o_s, new_caches_t = outs[0], outs[1:]
    c_s, new_state_t = _decode_conv(state_conv[i].transpose(1, 0, 2), u_s.transpose(1, 0, 2), w_dw[i], row(b_dw[i]))
    y_s = _post(
        x_sample.reshape(1, n_s, D_MODEL), c_s.transpose(1, 0, 2).reshape(1, n_s, D_CONV),
        o_s.reshape(1, n_s, D_GROUP), p_sample[i].reshape(1, n_s, D_PLE), post_weights, tm=n_s,
        conv_in_kernel=False,
    ).reshape(bd, t_new, D_MODEL)
    conv_s = new_state_t.transpose(1, 0, 2)
    kv_caches_s = [
        t.reshape(bd, 2, HEADS_PER_GROUP, HEAD_DIM, t.shape[-1]).transpose(0, 4, 1, 2, 3) for t in new_caches_t
    ]
    return y_p, y_s, conv_p, conv_s, kv_caches_p, kv_caches_s


def kernel(x_prompt, x_sample, state_conv, cache_kv_g0, cache_kv_g1, cache_kv_g2, p_prompt, p_sample, g_mix, w_in,
           w_dw, b_dw, ln_g, ln_b, w_conv_out, w_att_out, w_out, g_ffn, w_ffn_gate, w_ffn_up, w_ffn_down, g_ple,
           w_ple_gate, w_ple, g_final):
    depth = w_in.shape[0]
    assert depth == 1
    y_p, y_s, conv_p, conv_s, kv_p, kv_s = _layer(
        0, x_prompt, x_sample, state_conv, (cache_kv_g0, cache_kv_g1, cache_kv_g2), p_prompt, p_sample, g_mix, w_in,
        w_dw, b_dw, ln_g, ln_b, w_conv_out, w_att_out, w_out, g_ffn, w_ffn_gate, w_ffn_up, w_ffn_down, g_ple,
        w_ple_gate, w_ple, g_final,
    )
    return (y_p, y_s, conv_p[None], conv_s[None], kv_p[0][None], kv_s[0][None], kv_p[1][None], kv_s[1][None],
            kv_p[2][None], kv_s[2][None])
```
